```python
import numpy as np
import jax
import jax.numpy as jnp
from jax import lax

D_MODEL = 2048
BATCH = 2
SEQ = 4096
DEPTH = 2
DEC_BATCH = 32
DEC_SEQ = 4
PAST_LEN = 8192
PAGE_SIZE = 128

N_BRANCH = 4
BRANCH_W = D_MODEL // 4
LRU_HEADS = 8
LRU_HD = BRANCH_W // LRU_HEADS
CONV_W = 4
LRU_C = 8.0
ATT_HEADS = 8
ATT_HD = BRANCH_W // ATT_HEADS
ATT_SCALE = ATT_HD ** -0.5
Q_BLOCK = 128
POOL_WINDOWS = (2, 4, 8, 16)
POOL_GROUPS = 4
POOL_GW = BRANCH_W // POOL_GROUPS
POOL_HIST = max(POOL_WINDOWS) - 1
SGU_CHUNK = 128
SGU_GROUPS = 4
SGU_GW = BRANCH_W // SGU_GROUPS
PEER_HEADS = 8
PEER_NKEYS = 128
PEER_N = PEER_NKEYS * PEER_NKEYS
PEER_QDIM = 256
PEER_HALF = PEER_QDIM // 2
PEER_TOPK = 16
PEER_TOK_BLOCK = 128
DN_ALPHA = (2 * DEPTH) ** 0.25
DN_BETA = (8 * DEPTH) ** -0.25
LN_EPS = 1e-5
IN_SIZES = (BRANCH_W, BRANCH_W, BRANCH_W, BRANCH_W, ATT_HEADS, BRANCH_W, BRANCH_W, BRANCH_W, N_BRANCH * D_MODEL)
N_IN = sum(IN_SIZES)

kernel_name = 'hybrid_rglru_fox_pool_sgu_peer_step'


def _layer_norm(x, g, b):
    xf = x.astype(jnp.float32)
    mu = jnp.mean(xf, axis=-1, keepdims=True)
    var = jnp.mean(jnp.square(xf - mu), axis=-1, keepdims=True)
    y = (xf - mu) * lax.rsqrt(var + LN_EPS)
    return (y * g.astype(jnp.float32) + b.astype(jnp.float32)).astype(x.dtype)


def _causal_conv(x, buf, w, b):
    t = x.shape[1]
    ext = jnp.concatenate([buf.astype(x.dtype), x], axis=1)
    y = b
    for j in range(CONV_W):
        y = y + w[j] * ext[:, j:j + t]
    return y, ext[:, -(CONV_W - 1):]


def _rg_lru(x, h0, wa, ba, wx, bx, lam):
    bsz, t, _ = x.shape
    xf = x.astype(jnp.float32)
    xh = xf.reshape(bsz, t, LRU_HEADS, LRU_HD)
    r = jax.nn.sigmoid(jnp.einsum('bthi,hij->bthj', xh, wa.astype(jnp.float32)).reshape(bsz, t, BRANCH_W) + ba.astype(jnp.float32))
    i = jax.nn.sigmoid(jnp.einsum('bthi,hij->bthj', xh, wx.astype(jnp.float32)).reshape(bsz, t, BRANCH_W) + bx.astype(jnp.float32))
    log_a = -LRU_C * r * jax.nn.softplus(-lam.astype(jnp.float32))
    a = jnp.exp(log_a)
    u = jnp.sqrt(-jnp.expm1(2.0 * log_a)) * (i * xf)

    def comb(c1, c2):
        a1, u1 = c1
        a2, u2 = c2
        return a1 * a2, a2 * u1 + u2

    a_cum, u_cum = lax.associative_scan(comb, (a, u), axis=1)
    h = a_cum * h0.astype(jnp.float32)[:, None, :] + u_cum
    return h.astype(x.dtype), h[:, -1].astype(x.dtype)


def _fox_block(q, k, v, cq, ck, qpos, kpos):
    s = jnp.einsum('bqhd,bkhd->bhqk', q, k).astype(jnp.float32) * ATT_SCALE
    s = s + jnp.swapaxes(cq, 1, 2)[:, :, :, None] - jnp.swapaxes(ck, 1, 2)[:, :, None, :]
    s = jnp.where((kpos[None, :] <= qpos[:, None])[None, None], s, -jnp.inf)
    p = jax.nn.softmax(s, axis=-1)
    return jnp.einsum('bhqk,bkhd->bqhd', p.astype(v.dtype), v)


def _fox_prompt(q, k, v, logf):
    bsz, s, h, d = q.shape
    c = jnp.cumsum(logf, axis=1)
    nb = s // Q_BLOCK
    qb = q.reshape(bsz, nb, Q_BLOCK, h, d).transpose(1, 0, 2, 3, 4)
    cb = c.reshape(bsz, nb, Q_BLOCK, h).transpose(1, 0, 2, 3)
    kpos = jnp.arange(s)

    def blk(args):
        bi, qi, ci = args
        qpos = bi * Q_BLOCK + jnp.arange(Q_BLOCK)
        return _fox_block(qi, k, v, ci, c, qpos, kpos)

    out = lax.map(blk, (jnp.arange(nb), qb, cb))
    return out.transpose(1, 0, 2, 3, 4).reshape(bsz, s, h * d)


def _fox_sample(q, k, v, logf, k_past, v_past, logf_past):
    bsz, t, h, d = q.shape
    p_len = k_past.shape[1]
    kk = jnp.concatenate([k_past.astype(k.dtype), k], axis=1)
    vv = jnp.concatenate([v_past.astype(v.dtype), v], axis=1)
    c = jnp.cumsum(jnp.concatenate([logf_past.astype(jnp.float32), logf], axis=1), axis=1)
    qpos = p_len + jnp.arange(t)
    kpos = jnp.arange(p_len + t)
    out = _fox_block(q, kk, vv, c[:, p_len:], c, qpos, kpos)
    return out.reshape(bsz, t, h * d)


def _multi_pool(x, buf, pos0, w, scale):
    bsz, t, cw = x.shape
    ext = jnp.concatenate([buf.astype(x.dtype), x], axis=1)
    cs = jnp.cumsum(ext.astype(jnp.float32), axis=1)
    cs = jnp.concatenate([jnp.zeros((bsz, 1, cw), jnp.float32), cs], axis=1)
    pos = pos0 + jnp.arange(t)
    outs = []
    for g, win in enumerate(POOL_WINDOWS):
        lo_c, hi_c = g * POOL_GW, (g + 1) * POOL_GW
        hi = cs[:, POOL_HIST + 1:POOL_HIST + 1 + t, lo_c:hi_c]
        lo = cs[:, POOL_HIST + 1 - win:POOL_HIST + 1 - win + t, lo_c:hi_c]
        cnt = jnp.minimum(win, pos + 1).astype(jnp.float32)
        outs.append((hi - lo) / cnt[None, :, None])
    pooled = jnp.concatenate(outs, axis=-1) - x.astype(jnp.float32)
    y = jnp.einsum('btgc,gcd->btgd', pooled.astype(x.dtype).reshape(bsz, t, POOL_GROUPS, POOL_GW), w)
    return y.reshape(bsz, t, cw) * scale, ext[:, -POOL_HIST:]


def _spatial_gate(v, w_s, b_s):
    bsz, t, cw = v.shape
    tp = -(-t // SGU_CHUNK) * SGU_CHUNK
    vp = jnp.pad(v, ((0, 0), (0, tp - t), (0, 0))).reshape(bsz, tp // SGU_CHUNK, SGU_CHUNK, SGU_GROUPS, SGU_GW)
    w = jnp.where(jnp.tril(jnp.ones((SGU_CHUNK, SGU_CHUNK), bool)), w_s, 0.0)
    mixed = jnp.einsum('gts,bnsgc->bntgc', w, vp) + jnp.transpose(b_s)[None, None, :, :, None]
    return mixed.reshape(bsz, tp, cw)[:, :t]


def _peer(x, wq, k1, k2, u_tab, v_tab):
    bsz, t, dm = x.shape
    n = bsz * t
    npad = -(-n // PEER_TOK_BLOCK) * PEER_TOK_BLOCK
    xt = jnp.pad(x.reshape(n, dm), ((0, npad - n), (0, 0))).reshape(npad // PEER_TOK_BLOCK, PEER_TOK_BLOCK, dm)

    def blk(xb):
        q = (xb @ wq).astype(jnp.float32).reshape(-1, PEER_HEADS, PEER_QDIM)
        s1 = jnp.einsum('thd,kd->thk', q[..., :PEER_HALF], k1.astype(jnp.float32))
        s2 = jnp.einsum('thd,kd->thk', q[..., PEER_HALF:], k2.astype(jnp.float32))
        v1, i1 = lax.top_k(s1, PEER_TOPK)
        v2, i2 = lax.top_k(s2, PEER_TOPK)
        cand = (v1[..., :, None] + v2[..., None, :]).reshape(v1.shape[0], PEER_HEADS, PEER_TOPK * PEER_TOPK)
        sc, ci = lax.top_k(cand, PEER_TOPK)
        e = jnp.take_along_axis(i1, ci // PEER_TOPK, axis=-1) * PEER_NKEYS + jnp.take_along_axis(i2, ci % PEER_TOPK, axis=-1)
        g = jax.nn.softmax(sc, axis=-1)
        act = jax.nn.gelu(jnp.einsum('thkd,td->thk', u_tab[e], xb).astype(jnp.float32), approximate=False)
        return jnp.einsum('thk,thkd->td', (g * act).astype(x.dtype), v_tab[e])

    y = lax.map(blk, xt).reshape(npad, dm)[:n]
    return y.reshape(bsz, t, dm)


def _trunk_layer(x, p, conv_buf, lru_h0, pool_buf, pos0, past):
    (w_in, conv_w, conv_b, lru_wa, lru_ba, lru_wx, lru_bx, lru_lambda, fox_bf, pool_w, pool_scale,
     sgu_ln_g, sgu_ln_b, sgu_w, sgu_b, w_branch, w_out, ln1_g, ln1_b,
     peer_wq, peer_k1, peer_k2, peer_u, peer_v, ln2_g, ln2_b) = p
    bsz, t, _ = x.shape
    splits = [int(s) for s in np.cumsum(IN_SIZES)[:-1]]
    xa, q, k, v, fl, xc, du, dv, gl = jnp.split(x @ w_in, splits, axis=-1)
    xa_c, conv_new = _causal_conv(xa, conv_buf, conv_w, conv_b)
    o_a, h_last = _rg_lru(xa_c, lru_h0, lru_wa, lru_ba, lru_wx, lru_bx, lru_lambda)
    q = q.reshape(bsz, t, ATT_HEADS, ATT_HD)
    k = k.reshape(bsz, t, ATT_HEADS, ATT_HD)
    v = v.reshape(bsz, t, ATT_HEADS, ATT_HD)
    logf = jax.nn.log_sigmoid((fl + fox_bf).astype(jnp.float32))
    if past is None:
        o_b = _fox_prompt(q, k, v, logf)
    else:
        o_b = _fox_sample(q, k, v, logf, past[0], past[1], past[2])
    o_c, pool_new = _multi_pool(xc, pool_buf, pos0, pool_w, pool_scale)
    vn = _layer_norm(jax.nn.gelu(dv, approximate=False), sgu_ln_g, sgu_ln_b)
    o_d = jax.nn.gelu(du, approximate=False) * _spatial_gate(vn, sgu_w, sgu_b)
    gates = jax.nn.sigmoid(gl.reshape(bsz, t, N_BRANCH, D_MODEL))
    merged = gates[:, :, 0] * (o_a @ w_branch[0])
    merged = merged + gates[:, :, 1] * (o_b @ w_branch[1])
    merged = merged + gates[:, :, 2] * (o_c @ w_branch[2])
    merged = merged + gates[:, :, 3] * (o_d @ w_branch[3])
    x = _layer_norm(DN_ALPHA * x + merged @ w_out, ln1_g, ln1_b)
    x = _layer_norm(DN_ALPHA * x + _peer(x, peer_wq, peer_k1, peer_k2, peer_u, peer_v), ln2_g, ln2_b)
    return x, (k, v, logf.astype(x.dtype), conv_new, h_last, pool_new, vn)


def setup_inputs(seed: int = 0) -> dict:
    key = jax.random.key(seed)
    ks = iter(jax.random.split(key, 48))

    def nrm(shape, scale):
        return jax.random.normal(next(ks), shape, jnp.float32) * scale

    n_pages = PAST_LEN // PAGE_SIZE
    n_pool = (DEC_BATCH * n_pages * 5) // 4
    d = D_MODEL
    x_prompt = nrm((BATCH, SEQ, d), 1.0)
    x_sample = nrm((DEC_BATCH, DEC_SEQ, d), 1.0)
    cache_k = nrm((DEPTH, n_pool, PAGE_SIZE, ATT_HEADS, ATT_HD), 1.0)
    cache_v = nrm((DEPTH, n_pool, PAGE_SIZE, ATT_HEADS, ATT_HD), 1.0)
    cache_logf = jax.nn.log_sigmoid(nrm((DEPTH, n_pool, PAGE_SIZE, ATT_HEADS), 1.0) + 3.0)
    state_conv = nrm((DEPTH, DEC_BATCH, CONV_W - 1, BRANCH_W), 1.0)
    state_lru = nrm((DEPTH, DEC_BATCH, BRANCH_W), 0.5)
    state_pool = nrm((DEPTH, DEC_BATCH, POOL_HIST, BRANCH_W), 1.0)
    page_table = jax.random.permutation(next(ks), n_pool)[:DEC_BATCH * n_pages].reshape(DEC_BATCH, n_pages).astype(jnp.int32)
    w_in = nrm((DEPTH, d, N_IN), d ** -0.5)
    conv_w = nrm((DEPTH, CONV_W, BRANCH_W), CONV_W ** -0.5)
    conv_b = nrm((DEPTH, BRANCH_W), 0.01)
    lru_wa = nrm((DEPTH, LRU_HEADS, LRU_HD, LRU_HD), LRU_HD ** -0.5)
    lru_ba = nrm((DEPTH, BRANCH_W), 0.01)
    lru_wx = nrm((DEPTH, LRU_HEADS, LRU_HD, LRU_HD), LRU_HD ** -0.5)
    lru_bx = nrm((DEPTH, BRANCH_W), 0.01)
    a0 = jax.random.uniform(next(ks), (DEPTH, BRANCH_W), jnp.float32, 0.9, 0.999) ** (1.0 / LRU_C)
    lru_lambda = jnp.log(a0) - jnp.log1p(-a0)
    fox_bf = jax.random.uniform(next(ks), (DEPTH, ATT_HEADS), jnp.float32, 1.0, 6.0)
    pool_w = nrm((DEPTH, POOL_GROUPS, POOL_GW, POOL_GW), POOL_GW ** -0.5)
    pool_scale = 1.0 + nrm((DEPTH, BRANCH_W), 0.1)
    sgu_ln_g = 1.0 + nrm((DEPTH, BRANCH_W), 0.1)
    sgu_ln_b = nrm((DEPTH, BRANCH_W), 0.01)
    sgu_w = nrm((DEPTH, SGU_GROUPS, SGU_CHUNK, SGU_CHUNK), SGU_CHUNK ** -0.5)
    sgu_b = 1.0 + nrm((DEPTH, SGU_GROUPS, SGU_CHUNK), 0.1)
    w_branch = nrm((DEPTH, N_BRANCH, BRANCH_W, d), BRANCH_W ** -0.5 * DN_BETA)
    w_out = nrm((DEPTH, d, d), d ** -0.5 * DN_BETA)
    ln1_g = 1.0 + nrm((DEPTH, d), 0.1)
    ln1_b = nrm((DEPTH, d), 0.01)
    peer_wq = nrm((DEPTH, d, PEER_HEADS * PEER_QDIM), d ** -0.5)
    peer_k1 = nrm((DEPTH, PEER_NKEYS, PEER_HALF), PEER_HALF ** -0.5)
    peer_k2 = nrm((DEPTH, PEER_NKEYS, PEER_HALF), PEER_HALF ** -0.5)
    peer_u = nrm((DEPTH, PEER_N, d), d ** -0.5)
    peer_v = nrm((DEPTH, PEER_N, d), DN_BETA)
    ln2_g = 1.0 + nrm((DEPTH, d), 0.1)
    ln2_b = nrm((DEPTH, d), 0.01)
    return {'x_prompt': x_prompt, 'x_sample': x_sample, 'cache_k': cache_k, 'cache_v': cache_v,
            'cache_logf': cache_logf, 'state_conv': state_conv, 'state_lru': state_lru,
            'state_pool': state_pool, 'page_table': page_table, 'w_in': w_in, 'conv_w': conv_w,
            'conv_b': conv_b, 'lru_wa': lru_wa, 'lru_ba': lru_ba, 'lru_wx': lru_wx, 'lru_bx': lru_bx,
            'lru_lambda': lru_lambda, 'fox_bf': fox_bf, 'pool_w': pool_w, 'pool_scale': pool_scale,
            'sgu_ln_g': sgu_ln_g, 'sgu_ln_b': sgu_ln_b, 'sgu_w': sgu_w, 'sgu_b': sgu_b,
            'w_branch': w_branch, 'w_out': w_out, 'ln1_g': ln1_g, 'ln1_b': ln1_b,
            'peer_wq': peer_wq, 'peer_k1': peer_k1, 'peer_k2': peer_k2, 'peer_u': peer_u,
            'peer_v': peer_v, 'ln2_g': ln2_g, 'ln2_b': ln2_b}


def reference(x_prompt, x_sample, cache_k, cache_v, cache_logf, state_conv, state_lru, state_pool,
              page_table, w_in, conv_w, conv_b, lru_wa, lru_ba, lru_wx, lru_bx, lru_lambda, fox_bf,
              pool_w, pool_scale, sgu_ln_g, sgu_ln_b, sgu_w, sgu_b, w_branch, w_out, ln1_g, ln1_b,
              peer_wq, peer_k1, peer_k2, peer_u, peer_v, ln2_g, ln2_b):
    weights = (w_in, conv_w, conv_b, lru_wa, lru_ba, lru_wx, lru_bx, lru_lambda, fox_bf, pool_w, pool_scale,
               sgu_ln_g, sgu_ln_b, sgu_w, sgu_b, w_branch, w_out, ln1_g, ln1_b,
               peer_wq, peer_k1, peer_k2, peer_u, peer_v, ln2_g, ln2_b)
    bp = x_prompt.shape[0]
    bs = x_sample.shape[0]
    yp, ys = x_prompt, x_sample
    pst = [[] for _ in range(6)]
    sst = [[] for _ in range(7)]
    for l in range(DEPTH):
        p = tuple(w[l] for w in weights)
        yp, stp = _trunk_layer(yp, p,
                               jnp.zeros((bp, CONV_W - 1, BRANCH_W), yp.dtype),
                               jnp.zeros((bp, BRANCH_W), yp.dtype),
                               jnp.zeros((bp, POOL_HIST, BRANCH_W), yp.dtype),
                               0, None)
        for j in range(6):
            pst[j].append(stp[j])
        k_past = cache_k[l, page_table].reshape(bs, -1, ATT_HEADS, ATT_HD)
        v_past = cache_v[l, page_table].reshape(bs, -1, ATT_HEADS, ATT_HD)
        lf_past = cache_logf[l, page_table].reshape(bs, -1, ATT_HEADS)
        ys, sts = _trunk_layer(ys, p, state_conv[l], state_lru[l], state_pool[l], PAST_LEN,
                               (k_past, v_past, lf_past))
        for j in range(7):
            sst[j].append(sts[j])
    p_k, p_v, p_logf, p_conv, p_lru, p_pool = [jnp.stack(a) for a in pst]
    s_k, s_v, s_logf, s_conv, s_lru, s_pool, s_sgu_v = [jnp.stack(a) for a in sst]
    return (yp, ys, p_k, p_v, p_logf, p_conv, p_lru, p_pool, s_k, s_v, s_logf, s_conv, s_lru, s_pool, s_sgu_v)
```

```python
import functools

import jax
import jax.numpy as jnp
from jax import lax
from jax.experimental import pallas as pl
from jax.experimental.pallas import tpu as pltpu

F32 = jnp.float32
BF16 = jnp.bfloat16

N_BRANCH = 4
N_HEADS = 8
HEAD_DIM = 64
CONV_W = 4
LRU_C = 8.0
POOL_WINDOWS = (2, 4, 8, 16)
POOL_HIST = 15
GROUP_W = 128
SGU_CHUNK = 128
PEER_NKEYS = 128
PEER_TOPK = 16
LN_EPS = 1e-5
NEG_INF = float("-inf")

VMEM_LIMIT_V7X = 56 * 1024 * 1024
LANES = 128


def _params(semantics):
    return pltpu.CompilerParams(dimension_semantics=semantics, vmem_limit_bytes=VMEM_LIMIT_V7X)


def _gelu(x):
    return 0.5 * x * (1.0 + lax.erf(x * 0.7071067811865476))


def _softplus(z):
    return jnp.maximum(z, 0.0) + jnp.log1p(jnp.exp(-jnp.abs(z)))


def _log_sigmoid(z):
    return -_softplus(-z)


def _expm1(y):
    e = jnp.exp(y)
    return jnp.where(e == 1.0, y, (e - 1.0) * y / jnp.log(e))


def _layer_norm(x, g, b):
    mu = jnp.mean(x, axis=-1, keepdims=True)
    d = x - mu
    var = jnp.mean(d * d, axis=-1, keepdims=True)
    return d * lax.rsqrt(var + LN_EPS) * g + b


def _cumsum_rows(x):
    n = x.shape[0]
    rows = lax.broadcasted_iota(jnp.int32, x.shape, 0)
    s = 1
    while s < n:
        x = x + jnp.where(rows >= s, pltpu.roll(x, s, axis=0), 0.0)
        s *= 2
    return x


def _dot_nt(a, b):
    return lax.dot_general(a, b, (((1,), (1,)), ((), ())), preferred_element_type=F32)


def _matmul_kernel(x_ref, w_ref, o_ref):
    o_ref[...] = jnp.dot(x_ref[...], w_ref[...], preferred_element_type=F32)


def _matmul(xb, w, tm, tn):
    m, k = xb.shape
    n = w.shape[1]
    return pl.pallas_call(
        _matmul_kernel,
        grid=(m // tm, n // tn),
        in_specs=[pl.BlockSpec((tm, k), lambda i, j: (i, 0)),
                  pl.BlockSpec((k, tn), lambda i, j: (0, j))],
        out_specs=pl.BlockSpec((tm, tn), lambda i, j: (i, j)),
        out_shape=jax.ShapeDtypeStruct((m, n), F32),
        compiler_params=_params(("parallel", "arbitrary")),
        name="in_proj",
    )(xb, w)


def _logf_kernel(x_ref, w_ref, b_ref, lf_ref, c_ref, carry_ref):
    i = pl.program_id(1)
    fl = jnp.dot(x_ref[...], w_ref[...], preferred_element_type=F32) + b_ref[...]
    lf = _log_sigmoid(fl)
    lf_ref[...] = lf

    @pl.when(i == 0)
    def _():
        carry_ref[...] = jnp.zeros_like(carry_ref)

    c = _cumsum_rows(lf) + carry_ref[...]
    c_ref[...] = c
    carry_ref[...] = c[c.shape[0] - 1:, :]


def _logf(xb, w_f, b_f, row0, n_seq, seq_len, tt):
    k = xb.shape[1]
    nt = seq_len // tt
    blk0 = row0 // tt
    rows = n_seq * seq_len
    return pl.pallas_call(
        _logf_kernel,
        grid=(n_seq, nt),
        in_specs=[pl.BlockSpec((tt, k), lambda b, i: (blk0 + b * nt + i, 0)),
                  pl.BlockSpec((k, LANES), lambda b, i: (0, 0)),
                  pl.BlockSpec((1, LANES), lambda b, i: (0, 0))],
        out_specs=[pl.BlockSpec((tt, LANES), lambda b, i: (b * nt + i, 0)),
                   pl.BlockSpec((tt, LANES), lambda b, i: (b * nt + i, 0))],
        out_shape=[jax.ShapeDtypeStruct((rows, LANES), F32),
                   jax.ShapeDtypeStruct((rows, LANES), F32)],
        scratch_shapes=[pltpu.VMEM((1, LANES), F32)],
        compiler_params=_params(("parallel", "arbitrary")),
        name="log_forget",
    )(xb, w_f, b_f)


HALO = 8


def _lru_gates(xc, wa, ba, wx, bx, lam):
    xcb = xc.astype(BF16)
    r = jax.nn.sigmoid(jnp.dot(xcb, wa, preferred_element_type=F32) + ba)
    ig = jax.nn.sigmoid(jnp.dot(xcb, wx, preferred_element_type=F32) + bx)
    log_a = -LRU_C * r * _softplus(-lam)
    a = jnp.exp(log_a)
    u = jnp.sqrt(-_expm1(2.0 * log_a)) * (ig * xc)
    return a, u


def _lru_prompt_kernel(xa_ref, cw_ref, cb_ref, wa_ref, ba_ref, wx_ref, bx_ref, lam_ref,
                       o_ref, ext_ref, h_ref, a_ref, u_ref):
    i = pl.program_id(1)
    tt = xa_ref.shape[0]

    @pl.when(i == 0)
    def _():
        ext_ref[0:HALO, :] = jnp.zeros((HALO, ext_ref.shape[1]), F32)
        h_ref[...] = jnp.zeros_like(h_ref)

    @pl.when(i > 0)
    def _():
        ext_ref[0:HALO, :] = ext_ref[tt:tt + HALO, :]

    ext_ref[HALO:HALO + tt, :] = xa_ref[...]
    cw = cw_ref[...]
    xc = cb_ref[...]
    for j in range(CONV_W):
        off = HALO - (CONV_W - 1) + j
        xc = xc + cw[j:j + 1, :] * ext_ref[off:off + tt, :]
    a, u = _lru_gates(xc, wa_ref[...], ba_ref[...], wx_ref[...], bx_ref[...], lam_ref[...])
    a_ref[...] = a
    u_ref[...] = u

    def body(t, h):
        h = a_ref[pl.ds(t, 1), :] * h + u_ref[pl.ds(t, 1), :]
        o_ref[pl.ds(t, 1), :] = h
        return h

    h_ref[...] = lax.fori_loop(0, tt, body, h_ref[...])


def _lru_prompt(proj, col_blk, n_seq, seq_len, tt, cw, cb, wa, ba, wx, bx, lam):
    c = cw.shape[1]
    nt = seq_len // tt
    const = lambda b, i: (0, 0)
    return pl.pallas_call(
        _lru_prompt_kernel,
        grid=(n_seq, nt),
        in_specs=[pl.BlockSpec((tt, c), lambda b, i: (b * nt + i, col_blk)),
                  pl.BlockSpec((CONV_W, c), const), pl.BlockSpec((1, c), const),
                  pl.BlockSpec((c, c), const), pl.BlockSpec((1, c), const),
                  pl.BlockSpec((c, c), const), pl.BlockSpec((1, c), const),
                  pl.BlockSpec((1, c), const)],
        out_specs=pl.BlockSpec((tt, c), lambda b, i: (b * nt + i, 0)),
        out_shape=jax.ShapeDtypeStruct((n_seq * seq_len, c), F32),
        scratch_shapes=[pltpu.VMEM((HALO + tt, c), F32), pltpu.VMEM((1, c), F32),
                        pltpu.VMEM((tt, c), F32), pltpu.VMEM((tt, c), F32)],
        compiler_params=_params(("parallel", "arbitrary")),
        name="lru_prompt",
    )(proj, cw, cb, wa, ba, wx, bx, lam)


POOL_HALO = 16


def _pool_prompt_kernel(x_ref, w_ref, sc_ref, o_ref, ext_ref):
    i = pl.program_id(1)
    tt = x_ref.shape[0]

    @pl.when(i == 0)
    def _():
        ext_ref[0:POOL_HALO, :] = jnp.zeros((POOL_HALO, ext_ref.shape[1]), F32)

    @pl.when(i > 0)
    def _():
        ext_ref[0:POOL_HALO, :] = ext_ref[tt:tt + POOL_HALO, :]

    ext_ref[POOL_HALO:POOL_HALO + tt, :] = x_ref[...]
    pos = i * tt + lax.broadcasted_iota(jnp.int32, (tt, GROUP_W), 0)
    for g, win in enumerate(POOL_WINDOWS):
        lo, hi = g * GROUP_W, (g + 1) * GROUP_W
        wsum = ext_ref[POOL_HALO:POOL_HALO + tt, lo:hi]
        for j in range(1, win):
            wsum = wsum + ext_ref[POOL_HALO - j:POOL_HALO - j + tt, lo:hi]
        cnt = jnp.minimum(win, pos + 1).astype(F32)
        pooled = wsum / cnt - x_ref[:, lo:hi]
        y = jnp.dot(pooled.astype(BF16), w_ref[g], preferred_element_type=F32)
        o_ref[:, lo:hi] = y * sc_ref[:, lo:hi]


def _pool_prompt(proj, col_blk, n_seq, seq_len, tt, w, scale):
    c = scale.shape[1]
    nt = seq_len // tt
    return pl.pallas_call(
        _pool_prompt_kernel,
        grid=(n_seq, nt),
        in_specs=[pl.BlockSpec((tt, c), lambda b, i: (b * nt + i, col_blk)),
                  pl.BlockSpec(w.shape, lambda b, i: (0, 0, 0)),
                  pl.BlockSpec((1, c), lambda b, i: (0, 0))],
        out_specs=pl.BlockSpec((tt, c), lambda b, i: (b * nt + i, 0)),
        out_shape=jax.ShapeDtypeStruct((n_seq * seq_len, c), F32),
        scratch_shapes=[pltpu.VMEM((POOL_HALO + tt, c), F32)],
        compiler_params=_params(("parallel", "arbitrary")),
        name="pool_prompt",
    )(proj, w, scale)


def _sgu_prompt_kernel(du_ref, dv_ref, g_ref, b_ref, w_ref, sbt_ref, o_ref):
    tt = du_ref.shape[0]
    vn = _layer_norm(_gelu(dv_ref[...]), g_ref[...], b_ref[...])
    vnb = vn.astype(BF16)
    gu = _gelu(du_ref[...])
    row = lax.broadcasted_iota(jnp.int32, (SGU_CHUNK, SGU_CHUNK), 0)
    col = lax.broadcasted_iota(jnp.int32, (SGU_CHUNK, SGU_CHUNK), 1)
    sbt = sbt_ref[...]
    for g in range(N_BRANCH):
        lo, hi = g * GROUP_W, (g + 1) * GROUP_W
        wt = jnp.where(col <= row, w_ref[g], 0.0).astype(BF16)
        bcol = sbt[:, g:g + 1]
        for ch in range(tt // SGU_CHUNK):
            r0, r1 = ch * SGU_CHUNK, (ch + 1) * SGU_CHUNK
            mixed = jnp.dot(wt, vnb[r0:r1, lo:hi], preferred_element_type=F32) + bcol
            o_ref[r0:r1, lo:hi] = gu[r0:r1, lo:hi] * mixed


def _sgu_prompt(proj, col_u, col_v, n_rows, tt, ln_g, ln_b, w, sbt):
    c = ln_g.shape[1]
    return pl.pallas_call(
        _sgu_prompt_kernel,
        grid=(n_rows // tt,),
        in_specs=[pl.BlockSpec((tt, c), lambda i: (i, col_u)),
                  pl.BlockSpec((tt, c), lambda i: (i, col_v)),
                  pl.BlockSpec((1, c), lambda i: (0, 0)),
                  pl.BlockSpec((1, c), lambda i: (0, 0)),
                  pl.BlockSpec(w.shape, lambda i: (0, 0, 0)),
                  pl.BlockSpec(sbt.shape, lambda i: (0, 0))],
        out_specs=pl.BlockSpec((tt, c), lambda i: (i, 0)),
        out_shape=jax.ShapeDtypeStruct((n_rows, c), F32),
        compiler_params=_params(("parallel",)),
        name="sgu_prompt",
    )(proj, proj, ln_g, ln_b, w, sbt)


def _fox_prompt_kernel(q_ref, k_ref, v_ref, cq_ref, ck_ref, o_ref, kb_ref, vb_ref,
                       m_ref, l_ref, acc_ref, *, scale, tk):
    p = pl.program_id(1)
    i = pl.program_id(2)
    tq = q_ref.shape[0]

    @pl.when(i == 0)
    def _():
        kb_ref[...] = k_ref[...].astype(BF16)
        vb_ref[...] = v_ref[...].astype(BF16)

    lane = lax.broadcasted_iota(jnp.int32, (tq, LANES), 1)
    q = q_ref[...] * scale
    cq_all = cq_ref[...]
    qrow = i * tq + lax.broadcasted_iota(jnp.int32, (tq, tk), 0)
    kcol = lax.broadcasted_iota(jnp.int32, (tq, tk), 1)
    outs = []
    for hh in range(2):
        head_lanes = (lane >= hh * HEAD_DIM) & (lane < (hh + 1) * HEAD_DIM)
        qh = jnp.where(head_lanes, q, 0.0).astype(BF16)
        h = 2 * p + hh
        cq = jnp.sum(jnp.where(lane == h, cq_all, 0.0), axis=1, keepdims=True)
        m_ref[...] = jnp.full(m_ref.shape, NEG_INF, F32)
        l_ref[...] = jnp.zeros_like(l_ref)
        acc_ref[...] = jnp.zeros_like(acc_ref)

        def body(j, carry):
            k0 = pl.multiple_of(j * tk, tk)
            s = _dot_nt(qh, kb_ref[pl.ds(k0, tk), :])
            ck = ck_ref[0, pl.ds(h, 1), pl.ds(k0, tk)]
            s = s + (cq - ck)
            s = jnp.where(k0 + kcol <= qrow, s, NEG_INF)
            m_old = m_ref[...]
            m_new = jnp.maximum(m_old, jnp.max(s, axis=1, keepdims=True))
            alpha = jnp.exp(m_old - m_new)
            pr = jnp.exp(s - m_new)
            l_ref[...] = alpha * l_ref[...] + jnp.sum(pr, axis=1, keepdims=True)
            acc_ref[...] = alpha * acc_ref[...] + jnp.dot(
                pr.astype(BF16), vb_ref[pl.ds(k0, tk), :], preferred_element_type=F32)
            m_ref[...] = m_new
            return carry

        n_kv = (i * tq + tq + tk - 1) // tk
        lax.fori_loop(0, n_kv, body, 0)
        outs.append(acc_ref[...] / l_ref[...])
    o_ref[...] = jnp.where(lane < HEAD_DIM, outs[0], outs[1])


def _fox_prompt(proj, c_rows, c_t, n_seq, seq_len, tq, tk, col_q, col_k, col_v):
    nq = seq_len // tq
    n_pairs = N_HEADS // 2
    kern = functools.partial(_fox_prompt_kernel, scale=HEAD_DIM ** -0.5, tk=tk)
    return pl.pallas_call(
        kern,
        grid=(n_seq, n_pairs, nq),
        in_specs=[pl.BlockSpec((tq, LANES), lambda b, p, i: (b * nq + i, col_q + p)),
                  pl.BlockSpec((seq_len, LANES), lambda b, p, i: (b, col_k + p)),
                  pl.BlockSpec((seq_len, LANES), lambda b, p, i: (b, col_v + p)),
                  pl.BlockSpec((tq, LANES), lambda b, p, i: (b * nq + i, 0)),
                  pl.BlockSpec((1, N_HEADS, seq_len), lambda b, p, i: (b, 0, 0))],
        out_specs=pl.BlockSpec((tq, LANES), lambda b, p, i: (b * nq + i, p)),
        out_shape=jax.ShapeDtypeStruct((n_seq * seq_len, n_pairs * LANES), F32),
        scratch_shapes=[pltpu.VMEM((seq_len, LANES), BF16), pltpu.VMEM((seq_len, LANES), BF16),
                        pltpu.VMEM((tq, 1), F32), pltpu.VMEM((tq, 1), F32),
                        pltpu.VMEM((tq, LANES), F32)],
        compiler_params=_params(("parallel", "parallel", "arbitrary")),
        name="fox_prompt",
    )(proj, proj, proj, c_rows, c_t)


def _stride_prefix(x, stride, cyclic):
    n = x.shape[1]
    lane = lax.broadcasted_iota(jnp.int32, x.shape, 1)
    s = stride
    while s < n:
        r = pltpu.roll(x, s, axis=1)
        x = x + (r if cyclic else jnp.where(lane >= s, r, 0.0))
        s *= 2
    return x


def _fox_sample_kernel(pt_ref, q_ref, k_ref, v_ref, lf_ref, kn_ref, vn_ref, lfn_ref, o_ref,
                       m_ref, l_ref, acc_ref, cb_ref, *, scale, n_new):
    j = pl.program_id(1)
    n_pages = pl.num_programs(1)
    rows = q_ref.shape[1]
    qb = (q_ref[0] * scale).astype(BF16)

    @pl.when(j == 0)
    def _():
        m_ref[...] = jnp.full(m_ref.shape, NEG_INF, F32)
        l_ref[...] = jnp.zeros_like(l_ref)
        acc_ref[...] = jnp.zeros_like(acc_ref)
        cb_ref[...] = jnp.zeros_like(cb_ref)

    def block(kk, vv, lf8, valid):
        nk = kk.shape[0]
        c_k = cb_ref[:, 0:nk] + _stride_prefix(lf8, N_HEADS, cyclic=False)
        s = _dot_nt(qb, kk.astype(BF16)) - c_k[0:1, :]
        s = jnp.where(valid, s, NEG_INF)
        m_old = m_ref[...]
        m_new = jnp.maximum(m_old, jnp.max(s, axis=1, keepdims=True))
        alpha = jnp.exp(m_old - m_new)
        pr = jnp.exp(s - m_new)
        l_ref[...] = alpha * l_ref[...] + jnp.sum(pr, axis=1, keepdims=True)
        acc_ref[...] = alpha * acc_ref[...] + jnp.dot(pr.astype(BF16), vv.astype(BF16),
                                                      preferred_element_type=F32)
        m_ref[...] = m_new

    nk_page = k_ref.shape[1]
    r_i = lax.broadcasted_iota(jnp.int32, (rows, nk_page), 0)
    c_i = lax.broadcasted_iota(jnp.int32, (rows, nk_page), 1)
    head_mask = N_HEADS - 1
    same_head = (r_i & head_mask) == (c_i & head_mask)
    lf8 = jnp.broadcast_to(lf_ref[0], (8, nk_page))
    block(k_ref[0], v_ref[0], lf8, same_head)
    cb_ref[...] = cb_ref[...] + _stride_prefix(lf8, N_HEADS, cyclic=True)

    @pl.when(j == n_pages - 1)
    def _():
        nk_new = kn_ref.shape[1]
        r_n = lax.broadcasted_iota(jnp.int32, (rows, nk_new), 0)
        c_n = lax.broadcasted_iota(jnp.int32, (rows, nk_new), 1)
        ok = ((r_n & head_mask) == (c_n & head_mask)) & (c_n <= (r_n | head_mask)) \
            & (c_n < n_new * N_HEADS)
        block(kn_ref[0], vn_ref[0], jnp.broadcast_to(lfn_ref[0], (8, nk_new)), ok)
        o_ref[0] = acc_ref[...] / l_ref[...]


def _fox_sample(page_table, q2, k_pages, v_pages, lf_pages, k_new, v_new, lf_new, n_new):
    db, rows, hd = q2.shape
    n_pages = page_table.shape[1]
    nk = k_pages.shape[1]
    nkn = k_new.shape[1]
    kern = functools.partial(_fox_sample_kernel, scale=HEAD_DIM ** -0.5, n_new=n_new)
    grid_spec = pltpu.PrefetchScalarGridSpec(
        num_scalar_prefetch=1,
        grid=(db, n_pages),
        in_specs=[pl.BlockSpec((1, rows, hd), lambda b, j, pt: (b, 0, 0)),
                  pl.BlockSpec((1, nk, hd), lambda b, j, pt: (pt[b, j], 0, 0)),
                  pl.BlockSpec((1, nk, hd), lambda b, j, pt: (pt[b, j], 0, 0)),
                  pl.BlockSpec((1, 1, nk), lambda b, j, pt: (pt[b, j], 0, 0)),
                  pl.BlockSpec((1, nkn, hd), lambda b, j, pt: (b, 0, 0)),
                  pl.BlockSpec((1, nkn, hd), lambda b, j, pt: (b, 0, 0)),
                  pl.BlockSpec((1, 1, nkn), lambda b, j, pt: (b, 0, 0))],
        out_specs=pl.BlockSpec((1, rows, hd), lambda b, j, pt: (b, 0, 0)),
        scratch_shapes=[pltpu.VMEM((rows, 1), F32), pltpu.VMEM((rows, 1), F32),
                        pltpu.VMEM((rows, hd), F32), pltpu.VMEM((8, nk), F32)],
    )
    return pl.pallas_call(
        kern,
        grid_spec=grid_spec,
        out_shape=jax.ShapeDtypeStruct((db, rows, hd), F32),
        compiler_params=_params(("parallel", "arbitrary")),
        name="fox_sample",
    )(page_table, q2, k_pages, v_pages, lf_pages, k_new, v_new, lf_new)


def _sample_mix_kernel(xa_ref, sconv_ref, h0_ref, xc_ref, spool_ref, du_ref, dv_ref,
                       cw_ref, cb_ref, wa_ref, ba_ref, wx_ref, bx_ref, lam_ref,
                       pw_ref, psc_ref, lng_ref, lnb_ref, sw_ref, sb_ref,
                       oa_ref, oc_ref, vn_ref, od_ref, *, pos0):
    nt = xa_ref.shape[0]
    cw = cw_ref[...]
    ext = [sconv_ref[j] for j in range(CONV_W - 1)] + [xa_ref[t] for t in range(nt)]
    h = h0_ref[...]
    for t in range(nt):
        xc = cb_ref[...]
        for j in range(CONV_W):
            xc = xc + cw[j:j + 1, :] * ext[t + j]
        a, u = _lru_gates(xc, wa_ref[...], ba_ref[...], wx_ref[...], bx_ref[...], lam_ref[...])
        h = a * h + u
        oa_ref[t] = h
    pext = [spool_ref[j] for j in range(POOL_HIST)] + [xc_ref[t] for t in range(nt)]
    for t in range(nt):
        x_t = pext[POOL_HIST + t]
        for g, win in enumerate(POOL_WINDOWS):
            lo, hi = g * GROUP_W, (g + 1) * GROUP_W
            wsum = x_t[:, lo:hi]
            for j in range(1, win):
                wsum = wsum + pext[POOL_HIST + t - j][:, lo:hi]
            cnt = float(min(win, pos0 + t + 1))
            pooled = wsum / cnt - x_t[:, lo:hi]
            y = jnp.dot(pooled.astype(BF16), pw_ref[g], preferred_element_type=F32)
            oc_ref[t, :, lo:hi] = y * psc_ref[:, lo:hi]
    vns = []
    for t in range(nt):
        vn = _layer_norm(_gelu(dv_ref[t]), lng_ref[...], lnb_ref[...])
        vn_ref[t] = vn
        vns.append(vn)
    for t in range(nt):
        mixed = sb_ref[t]
        for s in range(t + 1):
            mixed = mixed + sw_ref[t, s] * vns[s]
        od_ref[t] = _gelu(du_ref[t]) * mixed


def _sample_mix(xa, sconv, h0, xc, spool, du, dv, cw, cb, wa, ba, wx, bx, lam,
                pw, psc, lng, lnb, sw, sb, pos0):
    shp = jax.ShapeDtypeStruct(xa.shape, F32)
    kern = functools.partial(_sample_mix_kernel, pos0=pos0)
    return pl.pallas_call(
        kern,
        out_shape=[shp, shp, shp, shp],
        compiler_params=pltpu.CompilerParams(vmem_limit_bytes=VMEM_LIMIT_V7X),
        name="sample_mix",
    )(xa, sconv, h0, xc, spool, du, dv, cw, cb, wa, ba, wx, bx, lam, pw, psc, lng, lnb, sw, sb)


def _merge_kernel(x_ref, oa_ref, ob_ref, oc_ref, od_ref, g0_ref, g1_ref, g2_ref, g3_ref,
                  wb_ref, o_ref):
    x = x_ref[...]
    acc = None
    for n, (o_n, g_n) in enumerate(((oa_ref, g0_ref), (ob_ref, g1_ref),
                                    (oc_ref, g2_ref), (od_ref, g3_ref))):
        gate = jax.nn.sigmoid(jnp.dot(x, g_n[...], preferred_element_type=F32))
        br = jnp.dot(o_n[...].astype(BF16), wb_ref[n], preferred_element_type=F32)
        acc = gate * br if acc is None else acc + gate * br
    o_ref[...] = acc.astype(o_ref.dtype)


def _merge(xb, oa, ob, oc, od, wg, wb, tm, tn):
    m, d = xb.shape
    c = oa.shape[1]
    ncol = d // tn
    o_spec = pl.BlockSpec((tm, c), lambda i, j: (i, 0))
    g_specs = [pl.BlockSpec((d, tn), (lambda n: (lambda i, j: (0, n * ncol + j)))(n))
               for n in range(N_BRANCH)]
    return pl.pallas_call(
        _merge_kernel,
        grid=(m // tm, ncol),
        in_specs=[pl.BlockSpec((tm, d), lambda i, j: (i, 0)), o_spec, o_spec, o_spec, o_spec]
        + g_specs + [pl.BlockSpec((N_BRANCH, c, tn), lambda i, j: (0, 0, j))],
        out_specs=pl.BlockSpec((tm, tn), lambda i, j: (i, j)),
        out_shape=jax.ShapeDtypeStruct((m, d), BF16),
        compiler_params=_params(("parallel", "arbitrary")),
        name="gated_merge",
    )(xb, oa, ob, oc, od, wg, wg, wg, wg, wb)


def _out_ln_kernel(m_ref, w_ref, x_ref, g_ref, b_ref, o_ref, ob_ref, *, alpha):
    y = jnp.dot(m_ref[...], w_ref[...], preferred_element_type=F32)
    out = _layer_norm(alpha * x_ref[...] + y, g_ref[...], b_ref[...])
    o_ref[...] = out
    ob_ref[...] = out.astype(BF16)


def _out_ln(merged, w, x, g, b, alpha, tm):
    m, d = x.shape
    row = pl.BlockSpec((tm, d), lambda i: (i, 0))
    vec = pl.BlockSpec((1, d), lambda i: (0, 0))
    return pl.pallas_call(
        functools.partial(_out_ln_kernel, alpha=alpha),
        grid=(m // tm,),
        in_specs=[row, pl.BlockSpec((d, d), lambda i: (0, 0)), row, vec, vec],
        out_specs=[row, row],
        out_shape=[jax.ShapeDtypeStruct((m, d), F32), jax.ShapeDtypeStruct((m, d), BF16)],
        compiler_params=_params(("parallel",)),
        name="out_proj_ln",
    )(merged, w, x, g, b)


def _add_ln_kernel(y_ref, x_ref, g_ref, b_ref, o_ref, ob_ref, *, alpha):
    out = _layer_norm(alpha * x_ref[...] + y_ref[...], g_ref[...], b_ref[...])
    o_ref[...] = out
    ob_ref[...] = out.astype(BF16)


def _add_ln(y, x, g, b, alpha, tm):
    m, d = x.shape
    row = pl.BlockSpec((tm, d), lambda i: (i, 0))
    vec = pl.BlockSpec((1, d), lambda i: (0, 0))
    return pl.pallas_call(
        functools.partial(_add_ln_kernel, alpha=alpha),
        grid=(m // tm,),
        in_specs=[row, row, vec, vec],
        out_specs=[row, row],
        out_shape=[jax.ShapeDtypeStruct((m, d), F32), jax.ShapeDtypeStruct((m, d), BF16)],
        compiler_params=_params(("parallel",)),
        name="peer_add_ln",
    )(y, x, g, b)


def _top_rows(s, k):
    vals = []
    for _ in range(k):
        mx = jnp.max(s, axis=0, keepdims=True)
        vals.append(mx)
        s = jnp.where(s == mx, NEG_INF, s)
    return vals


def _peer_score_kernel(x_ref, wq_ref, k1_ref, k2_ref, s1_ref, s2_ref, st_ref):
    half = k1_ref.shape[1]
    tm = x_ref.shape[0]
    q = jnp.dot(x_ref[...], wq_ref[...], preferred_element_type=F32)
    s1 = _dot_nt(k1_ref[...], q[:, :half].astype(BF16))
    s2 = _dot_nt(k2_ref[...], q[:, half:].astype(BF16))
    s1_ref[0] = s1
    s2_ref[0] = s2
    for c0 in range(0, tm, LANES):
        v1 = _top_rows(s1[:, c0:c0 + LANES], PEER_TOPK)
        v2 = _top_rows(s2[:, c0:c0 + LANES], PEER_TOPK)
        v2cat = jnp.concatenate(v2, axis=0)
        cand = jnp.concatenate([v1[a] + v2cat for a in range(PEER_TOPK)], axis=0)
        top = _top_rows(cand, PEER_TOPK)
        m = top[0]
        z = jnp.exp(top[0] - m)
        for kk in range(1, PEER_TOPK):
            z = z + jnp.exp(top[kk] - m)
        zero = jnp.zeros_like(m)
        st_ref[0, :, c0:c0 + LANES] = jnp.concatenate(
            [top[PEER_TOPK - 1], v1[0], v2[0], 1.0 / z, zero, zero, zero, zero], axis=0)


def _peer_scores(xb, wq, k1, k2, tm):
    n, d = xb.shape
    nk, half = k1.shape
    big = pl.BlockSpec((1, nk, tm), lambda i, h: (h, 0, i))
    return pl.pallas_call(
        _peer_score_kernel,
        grid=(n // tm, N_HEADS),
        in_specs=[pl.BlockSpec((tm, d), lambda i, h: (i, 0)),
                  pl.BlockSpec((d, 2 * half), lambda i, h: (0, h)),
                  pl.BlockSpec(k1.shape, lambda i, h: (0, 0)),
                  pl.BlockSpec(k2.shape, lambda i, h: (0, 0))],
        out_specs=[big, big, pl.BlockSpec((1, 8, tm), lambda i, h: (h, 0, i))],
        out_shape=[jax.ShapeDtypeStruct((N_HEADS, nk, n), F32),
                   jax.ShapeDtypeStruct((N_HEADS, nk, n), F32),
                   jax.ShapeDtypeStruct((N_HEADS, 8, n), F32)],
        compiler_params=_params(("parallel", "arbitrary")),
        name="peer_scores",
    )(xb, wq, k1, k2)


def _peer_dense_kernel(x_ref, u_ref, vt_ref, s1_ref, s2_ref, st_ref, y_ref,
                       acc_ref, e1_ref, e2_ref, g_ref):
    j = pl.program_id(1)
    n_e = pl.num_programs(1)
    te = u_ref.shape[0]
    nk = s2_ref.shape[1]

    @pl.when(j == 0)
    def _():
        acc_ref[...] = jnp.zeros_like(acc_ref)
        for h in range(N_HEADS):
            e1_ref[h] = jnp.exp(s1_ref[h] - st_ref[h, 1:2, :])
            e2_ref[h] = jnp.exp(s2_ref[h] - st_ref[h, 2:3, :]) * st_ref[h, 3:4, :]

    act = _gelu(_dot_nt(u_ref[...], x_ref[...]))
    for r in range(te // nk):
        i1 = j * (te // nk) + r
        w = None
        for h in range(N_HEADS):
            tot = s1_ref[h, pl.ds(i1, 1), :] + s2_ref[h]
            term = jnp.where(tot >= st_ref[h, 0:1, :],
                             e1_ref[h, pl.ds(i1, 1), :] * e2_ref[h], 0.0)
            w = term if w is None else w + term
        g_ref[r * nk:(r + 1) * nk, :] = (act[r * nk:(r + 1) * nk, :] * w).astype(BF16)
    acc_ref[...] += jnp.dot(vt_ref[...], g_ref[...], preferred_element_type=F32)

    @pl.when(j == n_e - 1)
    def _():
        y_ref[...] = acc_ref[...].T


def _peer_dense(xb, u, vt, s1t, s2t, st, row0, n_rows, tm, te):
    d = xb.shape[1]
    n_exp = u.shape[0]
    nk = s1t.shape[1]
    blk0 = row0 // tm
    tok3 = lambda i, j: (0, 0, blk0 + i)
    return pl.pallas_call(
        _peer_dense_kernel,
        grid=(n_rows // tm, n_exp // te),
        in_specs=[pl.BlockSpec((tm, d), lambda i, j: (blk0 + i, 0)),
                  pl.BlockSpec((te, d), lambda i, j: (j, 0)),
                  pl.BlockSpec((d, te), lambda i, j: (0, j)),
                  pl.BlockSpec((N_HEADS, nk, tm), tok3),
                  pl.BlockSpec((N_HEADS, nk, tm), tok3),
                  pl.BlockSpec((N_HEADS, 8, tm), tok3)],
        out_specs=pl.BlockSpec((tm, d), lambda i, j: (i, 0)),
        out_shape=jax.ShapeDtypeStruct((n_rows, d), F32),
        scratch_shapes=[pltpu.VMEM((d, tm), F32),
                        pltpu.VMEM((N_HEADS, nk, tm), F32), pltpu.VMEM((N_HEADS, nk, tm), F32),
                        pltpu.VMEM((te, tm), BF16)],
        compiler_params=_params(("parallel", "arbitrary")),
        name="peer_dense",
    )(xb, u, vt, s1t, s2t, st)


def _largest_divisor(n, cap, mult):
    best = None
    for t in range(mult, cap + 1, mult):
        if n % t == 0:
            best = t
    assert best is not None, (n, cap, mult)
    return best


def _block_diag(w):
    h, a, b = w.shape
    eye = jnp.eye(h, dtype=w.dtype)
    return (eye[:, None, :, None] * w[:, :, None, :]).reshape(h * a, h * b)


def _layer(depth, x, xb, n_seq, seq_len, db, dt, cache_k, cache_v, cache_logf, state_conv, state_lru,
           state_pool, page_table, p):
    (w_in, conv_w, conv_b, lru_wa, lru_ba, lru_wx, lru_bx, lru_lambda, fox_bf, pool_w, pool_scale,
     sgu_ln_g, sgu_ln_b, sgu_w, sgu_b, w_branch, w_out, ln1_g, ln1_b,
     peer_wq, peer_k1, peer_k2, peer_u, peer_v, ln2_g, ln2_b) = p
    n_tok, d = x.shape
    n_p = n_seq * seq_len
    n_s = db * dt
    c = d // N_BRANCH
    alpha = (2 * depth) ** 0.25
    past_len = page_table.shape[1] * cache_k.shape[1]
    assert seq_len % SGU_CHUNK == 0 and dt <= SGU_CHUNK and dt * N_HEADS <= LANES

    n_mix = 7 * c
    fcol = 4 * c
    w1 = jnp.concatenate([w_in[:, :fcol], w_in[:, fcol + N_HEADS:n_mix + N_HEADS]], axis=1).astype(BF16)
    w_f = jnp.pad(w_in[:, fcol:fcol + N_HEADS], ((0, 0), (0, LANES - N_HEADS))).astype(BF16)
    b_f = jnp.pad(fox_bf, (0, LANES - N_HEADS)).reshape(1, LANES)
    wg = w_in[:, n_mix + N_HEADS:].astype(BF16)
    wb = w_branch.astype(BF16)
    wo = w_out.astype(BF16)
    wa = _block_diag(lru_wa).astype(BF16)
    wx = _block_diag(lru_wx).astype(BF16)
    row = lambda v: v.reshape(1, -1)
    cb, ba, bx, lam = row(conv_b), row(lru_ba), row(lru_bx), row(lru_lambda)
    pw = pool_w.astype(BF16)
    psc, lng, lnb = row(pool_scale), row(sgu_ln_g), row(sgu_ln_b)

    tm_big = _largest_divisor(n_tok, 1100, 16)
    tm_ln = _largest_divisor(n_tok, 600, 16)

    proj = _matmul(xb, w1, tm_big, _largest_divisor(n_mix, 512, LANES))
    cA, cQ, cK, cV, cC, cU, cDV = (i * c for i in range(7))

    tt = _largest_divisor(seq_len, 512, 8)
    lf_p, c_p = _logf(xb, w_f, b_f, 0, n_seq, seq_len, tt)
    lf_s, _ = _logf(xb, w_f, b_f, n_p, 1, n_s, n_s)
    c_t = c_p[:, :N_HEADS].reshape(n_seq, seq_len, N_HEADS).transpose(0, 2, 1)

    oa_p = _lru_prompt(proj, cA // c, n_seq, seq_len, tt, conv_w, cb, wa, ba, wx, bx, lam)
    oc_p = _pool_prompt(proj, cC // c, n_seq, seq_len, tt, pw, psc)
    od_p = _sgu_prompt(proj, cU // c, cDV // c, n_p, tt, lng, lnb, sgu_w, sgu_b.T)
    ob_p = _fox_prompt(proj, c_p, c_t, n_seq, seq_len, tt, tt, cQ // LANES, cK // LANES, cV // LANES)

    ps = proj[n_p:]
    tmaj = lambda a: a.reshape(db, dt, c).transpose(1, 0, 2)
    sw = jnp.repeat(sgu_w[:, :dt, :dt].transpose(1, 2, 0), GROUP_W, axis=2).reshape(dt, dt, 1, c)
    sb = jnp.repeat(sgu_b[:, :dt].T, GROUP_W, axis=1).reshape(dt, 1, c)
    oa_s, oc_s, vn_s, od_s = _sample_mix(
        tmaj(ps[:, cA:cA + c]), state_conv.transpose(1, 0, 2), state_lru,
        tmaj(ps[:, cC:cC + c]), state_pool.transpose(1, 0, 2),
        tmaj(ps[:, cU:cU + c]), tmaj(ps[:, cDV:cDV + c]),
        conv_w, cb, wa, ba, wx, bx, lam, pw, psc, lng, lnb, sw, sb, past_len)
    bmaj = lambda a: a.transpose(1, 0, 2).reshape(n_s, c)

    n_pool, page = cache_k.shape[0], cache_k.shape[1]
    rows_q = dt * N_HEADS
    to2 = lambda a: a.reshape(db, rows_q, HEAD_DIM)
    pad_new = lambda a: jnp.pad(to2(a), ((0, 0), (0, LANES - rows_q), (0, 0)))
    lf_new = jnp.pad(lf_s[:, :N_HEADS].reshape(db, 1, rows_q), ((0, 0), (0, 0), (0, LANES - rows_q)))
    ob_s = _fox_sample(
        page_table, to2(ps[:, cQ:cQ + c]),
        cache_k.reshape(n_pool, page * N_HEADS, HEAD_DIM),
        cache_v.reshape(n_pool, page * N_HEADS, HEAD_DIM),
        cache_logf.reshape(n_pool, 1, page * N_HEADS),
        pad_new(ps[:, cK:cK + c]), pad_new(ps[:, cV:cV + c]), lf_new, dt).reshape(n_s, c)

    cat = lambda a, b: jnp.concatenate([a, b], axis=0)
    merged = _merge(xb, cat(oa_p, bmaj(oa_s)), cat(ob_p, ob_s), cat(oc_p, bmaj(oc_s)),
                    cat(od_p, bmaj(od_s)), wg, wb, tm_big, 256)
    x1, x1b = _out_ln(merged, wo, x, row(ln1_g), row(ln1_b), alpha, tm_ln)

    s1t, s2t, st = _peer_scores(x1b, peer_wq.astype(BF16), peer_k1.astype(BF16),
                                peer_k2.astype(BF16), _largest_divisor(n_tok, 640, LANES))
    u = peer_u.astype(BF16)
    vt = peer_v.T.astype(BF16)
    te = 512
    y_p = _peer_dense(x1b, u, vt, s1t, s2t, st, 0, n_p, _largest_divisor(n_p, 512, LANES), te)
    y_s = _peer_dense(x1b, u, vt, s1t, s2t, st, n_p, n_s, n_s, te)
    x2, x2b = _add_ln(cat(y_p, y_s), x1, row(ln2_g), row(ln2_b), alpha, tm_ln)

    pj = proj[:n_p].reshape(n_seq, seq_len, -1)
    hd = (N_HEADS, HEAD_DIM)
    p_state = (pj[:, :, cK:cK + c].reshape(n_seq, seq_len, *hd),
               pj[:, :, cV:cV + c].reshape(n_seq, seq_len, *hd),
               lf_p[:, :N_HEADS].reshape(n_seq, seq_len, N_HEADS),
               pj[:, seq_len - (CONV_W - 1):, cA:cA + c],
               oa_p.reshape(n_seq, seq_len, c)[:, -1],
               pj[:, seq_len - POOL_HIST:, cC:cC + c])
    sj = ps.reshape(db, dt, -1)
    s_state = (sj[:, :, cK:cK + c].reshape(db, dt, *hd),
               sj[:, :, cV:cV + c].reshape(db, dt, *hd),
               lf_s[:, :N_HEADS].reshape(db, dt, N_HEADS),
               jnp.concatenate([state_conv, sj[:, :, cA:cA + c]], axis=1)[:, -(CONV_W - 1):],
               oa_s[dt - 1],
               jnp.concatenate([state_pool, sj[:, :, cC:cC + c]], axis=1)[:, -POOL_HIST:],
               vn_s.transpose(1, 0, 2))
    return x2, x2b, p_state, s_state


def kernel(x_prompt, x_sample, cache_k, cache_v, cache_logf, state_conv, state_lru, state_pool, page_table, w_in, conv_w, conv_b, lru_wa, lru_ba, lru_wx, lru_bx, lru_lambda, fox_bf, pool_w, pool_scale, sgu_ln_g, sgu_ln_b, sgu_w, sgu_b, w_branch, w_out, ln1_g, ln1_b, peer_wq, peer_k1, peer_k2, peer_u, peer_v, ln2_g, ln2_b):
    weights = (w_in, conv_w, conv_b, lru_wa, lru_ba, lru_wx, lru_bx, lru_lambda, fox_bf, pool_w,
               pool_scale, sgu_ln_g, sgu_ln_b, sgu_w, sgu_b, w_branch, w_out, ln1_g, ln1_b,
               peer_wq, peer_k1, peer_k2, peer_u, peer_v, ln2_g, ln2_b)
    n_seq, seq_len, d = x_prompt.shape
    db, dt, _ = x_sample.shape
    depth = w_in.shape[0]
    x = jnp.concatenate([x_prompt.reshape(n_seq * seq_len, d), x_sample.reshape(db * dt, d)], axis=0)
    xb = x.astype(BF16)
    p_states, s_states = [], []
    for l in range(depth):
        p = tuple(w[l] for w in weights)
        x, xb, ps, ss = _layer(depth, x, xb, n_seq, seq_len, db, dt, cache_k[l], cache_v[l],
                               cache_logf[l], state_conv[l], state_lru[l], state_pool[l],
                               page_table, p)
        p_states.append(ps)
        s_states.append(ss)
    n_p = n_seq * seq_len
    yp = x[:n_p].reshape(n_seq, seq_len, d)
    ys = x[n_p:].reshape(db, dt, d)
    p_out = [jnp.stack([st[j] for st in p_states]) for j in range(6)]
    s_out = [jnp.stack([st[j] for st in s_states]) for j in range(7)]
    return (yp, ys, *p_out, *s_out)
```

```python
import functools

import jax
import jax.numpy as jnp
from jax import lax
from jax.experimental import pallas as pl
from jax.experimental.pallas import tpu as pltpu

F32 = jnp.float32
BF16 = jnp.bfloat16

N_BRANCH = 4
N_HEADS = 8
HEAD_DIM = 64
CONV_W = 4
LRU_C = 8.0
POOL_WINDOWS = (2, 4, 8, 16)
POOL_HIST = 15
GROUP_W = 128
SGU_CHUNK = 128
PEER_NKEYS = 128
PEER_TOPK = 16
LN_EPS = 1e-5
NEG_INF = float("-inf")

VMEM_LIMIT_V7X = 56 * 1024 * 1024
LANES = 128


def _params(semantics):
    return pltpu.CompilerParams(dimension_semantics=semantics, vmem_limit_bytes=VMEM_LIMIT_V7X)


def _gelu(x):
    return 0.5 * x * (1.0 + lax.erf(x * 0.7071067811865476))


def _softplus(z):
    return jnp.maximum(z, 0.0) + jnp.log1p(jnp.exp(-jnp.abs(z)))


def _log_sigmoid(z):
    return -_softplus(-z)


def _expm1(y):
    e = jnp.exp(y)
    return jnp.where(e == 1.0, y, (e - 1.0) * y / jnp.log(e))


def _layer_norm(x, g, b):
    mu = jnp.mean(x, axis=-1, keepdims=True)
    d = x - mu
    var = jnp.mean(d * d, axis=-1, keepdims=True)
    return d * lax.rsqrt(var + LN_EPS) * g + b


def _cumsum_rows(x):
    n = x.shape[0]
    rows = lax.broadcasted_iota(jnp.int32, x.shape, 0)
    s = 1
    while s < n:
        x = x + jnp.where(rows >= s, pltpu.roll(x, s, axis=0), 0.0)
        s *= 2
    return x


def _dot_nt(a, b):
    return lax.dot_general(a, b, (((1,), (1,)), ((), ())), preferred_element_type=F32)


def _matmul_kernel(x_ref, w_ref, o_ref):
    o_ref[...] = jnp.dot(x_ref[...], w_ref[0], preferred_element_type=F32)


def _matmul(xb, w_all, layer, n_cols, tm, tn):
    m, k = xb.shape
    return pl.pallas_call(
        _matmul_kernel,
        grid=(m // tm, n_cols // tn),
        in_specs=[pl.BlockSpec((tm, k), lambda i, j: (i, 0)),
                  pl.BlockSpec((1, k, tn), lambda i, j: (layer, 0, j))],
        out_specs=pl.BlockSpec((tm, tn), lambda i, j: (i, j)),
        out_shape=jax.ShapeDtypeStruct((m, n_cols), F32),
        compiler_params=_params(("parallel", "arbitrary")),
        name="in_proj",
    )(xb, w_all)


def _logf_kernel(x_ref, w_ref, b_ref, lf_ref, c_ref, carry_ref):
    i = pl.program_id(1)
    fl = jnp.dot(x_ref[...], w_ref[...], preferred_element_type=F32) + b_ref[...]
    lf = _log_sigmoid(fl)
    lf_ref[...] = lf

    @pl.when(i == 0)
    def _():
        carry_ref[...] = jnp.zeros_like(carry_ref)

    c = _cumsum_rows(lf) + carry_ref[...]
    c_ref[...] = c
    carry_ref[...] = c[c.shape[0] - 1:, :]


def _logf(xb, w_f, b_f, n_seq, seq_len, tt):
    k = xb.shape[1]
    nt = seq_len // tt
    rows = n_seq * seq_len
    return pl.pallas_call(
        _logf_kernel,
        grid=(n_seq, nt),
        in_specs=[pl.BlockSpec((tt, k), lambda b, i: (b * nt + i, 0)),
                  pl.BlockSpec((k, LANES), lambda b, i: (0, 0)),
                  pl.BlockSpec((1, LANES), lambda b, i: (0, 0))],
        out_specs=[pl.BlockSpec((tt, LANES), lambda b, i: (b * nt + i, 0)),
                   pl.BlockSpec((tt, LANES), lambda b, i: (b * nt + i, 0))],
        out_shape=[jax.ShapeDtypeStruct((rows, LANES), F32),
                   jax.ShapeDtypeStruct((rows, LANES), F32)],
        scratch_shapes=[pltpu.VMEM((1, LANES), F32)],
        compiler_params=_params(("parallel", "arbitrary")),
        name="log_forget",
    )(xb, w_f, b_f)


HALO = 8


def _lru_gates(xc, wa, ba, wx, bx, lam):
    xcb = xc.astype(BF16)
    r = jax.nn.sigmoid(jnp.dot(xcb, wa, preferred_element_type=F32) + ba)
    ig = jax.nn.sigmoid(jnp.dot(xcb, wx, preferred_element_type=F32) + bx)
    log_a = -LRU_C * r * _softplus(-lam)
    a = jnp.exp(log_a)
    u = jnp.sqrt(-_expm1(2.0 * log_a)) * (ig * xc)
    return a, u


def _lru_prompt_kernel(xa_ref, cw_ref, cb_ref, wa_ref, ba_ref, wx_ref, bx_ref, lam_ref,
                       o_ref, ext_ref, h_ref, a_ref, u_ref):
    i = pl.program_id(1)
    tt = xa_ref.shape[0]

    @pl.when(i == 0)
    def _():
        ext_ref[0:HALO, :] = jnp.zeros((HALO, ext_ref.shape[1]), F32)
        h_ref[...] = jnp.zeros_like(h_ref)

    @pl.when(i > 0)
    def _():
        ext_ref[0:HALO, :] = ext_ref[tt:tt + HALO, :]

    ext_ref[HALO:HALO + tt, :] = xa_ref[...]
    cw = cw_ref[...]
    xc = cb_ref[...]
    for j in range(CONV_W):
        off = HALO - (CONV_W - 1) + j
        xc = xc + cw[j:j + 1, :] * ext_ref[off:off + tt, :]
    a, u = _lru_gates(xc, wa_ref[...], ba_ref[...], wx_ref[...], bx_ref[...], lam_ref[...])
    a_ref[...] = a
    u_ref[...] = u

    def body(t, h):
        h = a_ref[pl.ds(t, 1), :] * h + u_ref[pl.ds(t, 1), :]
        o_ref[pl.ds(t, 1), :] = h
        return h

    h_ref[...] = lax.fori_loop(0, tt, body, h_ref[...])


def _lru_prompt(proj, col_blk, n_seq, seq_len, tt, cw, cb, wa, ba, wx, bx, lam):
    c = cw.shape[1]
    nt = seq_len // tt
    const = lambda b, i: (0, 0)
    return pl.pallas_call(
        _lru_prompt_kernel,
        grid=(n_seq, nt),
        in_specs=[pl.BlockSpec((tt, c), lambda b, i: (b * nt + i, col_blk)),
                  pl.BlockSpec((CONV_W, c), const), pl.BlockSpec((1, c), const),
                  pl.BlockSpec((c, c), const), pl.BlockSpec((1, c), const),
                  pl.BlockSpec((c, c), const), pl.BlockSpec((1, c), const),
                  pl.BlockSpec((1, c), const)],
        out_specs=pl.BlockSpec((tt, c), lambda b, i: (b * nt + i, 0)),
        out_shape=jax.ShapeDtypeStruct((n_seq * seq_len, c), F32),
        scratch_shapes=[pltpu.VMEM((HALO + tt, c), F32), pltpu.VMEM((1, c), F32),
                        pltpu.VMEM((tt, c), F32), pltpu.VMEM((tt, c), F32)],
        compiler_params=_params(("parallel", "arbitrary")),
        name="lru_prompt",
    )(proj, cw, cb, wa, ba, wx, bx, lam)


POOL_HALO = 16


def _pool_prompt_kernel(x_ref, w_ref, sc_ref, o_ref, ext_ref):
    i = pl.program_id(1)
    tt = x_ref.shape[0]

    @pl.when(i == 0)
    def _():
        ext_ref[0:POOL_HALO, :] = jnp.zeros((POOL_HALO, ext_ref.shape[1]), F32)

    @pl.when(i > 0)
    def _():
        ext_ref[0:POOL_HALO, :] = ext_ref[tt:tt + POOL_HALO, :]

    ext_ref[POOL_HALO:POOL_HALO + tt, :] = x_ref[...]
    pos = i * tt + lax.broadcasted_iota(jnp.int32, (tt, GROUP_W), 0)
    for g, win in enumerate(POOL_WINDOWS):
        lo, hi = g * GROUP_W, (g + 1) * GROUP_W
        wsum = ext_ref[POOL_HALO:POOL_HALO + tt, lo:hi]
        for j in range(1, win):
            wsum = wsum + ext_ref[POOL_HALO - j:POOL_HALO - j + tt, lo:hi]
        cnt = jnp.minimum(win, pos + 1).astype(F32)
        pooled = wsum / cnt - x_ref[:, lo:hi]
        y = jnp.dot(pooled.astype(BF16), w_ref[g], preferred_element_type=F32)
        o_ref[:, lo:hi] = y * sc_ref[:, lo:hi]


def _pool_prompt(proj, col_blk, n_seq, seq_len, tt, w, scale):
    c = scale.shape[1]
    nt = seq_len // tt
    return pl.pallas_call(
        _pool_prompt_kernel,
        grid=(n_seq, nt),
        in_specs=[pl.BlockSpec((tt, c), lambda b, i: (b * nt + i, col_blk)),
                  pl.BlockSpec(w.shape, lambda b, i: (0, 0, 0)),
                  pl.BlockSpec((1, c), lambda b, i: (0, 0))],
        out_specs=pl.BlockSpec((tt, c), lambda b, i: (b * nt + i, 0)),
        out_shape=jax.ShapeDtypeStruct((n_seq * seq_len, c), F32),
        scratch_shapes=[pltpu.VMEM((POOL_HALO + tt, c), F32)],
        compiler_params=_params(("parallel", "arbitrary")),
        name="pool_prompt",
    )(proj, w, scale)


def _sgu_prompt_kernel(du_ref, dv_ref, g_ref, b_ref, w_ref, sbt_ref, o_ref):
    tt = du_ref.shape[0]
    vn = _layer_norm(_gelu(dv_ref[...]), g_ref[...], b_ref[...])
    vnb = vn.astype(BF16)
    gu = _gelu(du_ref[...])
    row = lax.broadcasted_iota(jnp.int32, (SGU_CHUNK, SGU_CHUNK), 0)
    col = lax.broadcasted_iota(jnp.int32, (SGU_CHUNK, SGU_CHUNK), 1)
    sbt = sbt_ref[...]
    for g in range(N_BRANCH):
        lo, hi = g * GROUP_W, (g + 1) * GROUP_W
        wt = jnp.where(col <= row, w_ref[g], 0.0).astype(BF16)
        bcol = sbt[:, g:g + 1]
        for ch in range(tt // SGU_CHUNK):
            r0, r1 = ch * SGU_CHUNK, (ch + 1) * SGU_CHUNK
            mixed = jnp.dot(wt, vnb[r0:r1, lo:hi], preferred_element_type=F32) + bcol
            o_ref[r0:r1, lo:hi] = gu[r0:r1, lo:hi] * mixed


def _sgu_prompt(proj, col_u, col_v, n_rows, tt, ln_g, ln_b, w, sbt):
    c = ln_g.shape[1]
    return pl.pallas_call(
        _sgu_prompt_kernel,
        grid=(n_rows // tt,),
        in_specs=[pl.BlockSpec((tt, c), lambda i: (i, col_u)),
                  pl.BlockSpec((tt, c), lambda i: (i, col_v)),
                  pl.BlockSpec((1, c), lambda i: (0, 0)),
                  pl.BlockSpec((1, c), lambda i: (0, 0)),
                  pl.BlockSpec(w.shape, lambda i: (0, 0, 0)),
                  pl.BlockSpec(sbt.shape, lambda i: (0, 0))],
        out_specs=pl.BlockSpec((tt, c), lambda i: (i, 0)),
        out_shape=jax.ShapeDtypeStruct((n_rows, c), F32),
        compiler_params=_params(("parallel",)),
        name="sgu_prompt",
    )(proj, proj, ln_g, ln_b, w, sbt)


def _fox_prompt_kernel(q_ref, k_ref, v_ref, cq_ref, ck_ref, o_ref, kb_ref, vb_ref,
                       m_ref, l_ref, acc_ref, *, scale, tk):
    p = pl.program_id(1)
    i = pl.program_id(2)
    tq = q_ref.shape[0]

    @pl.when(i == 0)
    def _():
        kb_ref[...] = k_ref[...].astype(BF16)
        vb_ref[...] = v_ref[...].astype(BF16)

    lane = lax.broadcasted_iota(jnp.int32, (tq, LANES), 1)
    q = q_ref[...] * scale
    cq_all = cq_ref[...]
    qrow = i * tq + lax.broadcasted_iota(jnp.int32, (tq, tk), 0)
    kcol = lax.broadcasted_iota(jnp.int32, (tq, tk), 1)
    outs = []
    for hh in range(2):
        head_lanes = (lane >= hh * HEAD_DIM) & (lane < (hh + 1) * HEAD_DIM)
        qh = jnp.where(head_lanes, q, 0.0).astype(BF16)
        h = 2 * p + hh
        cq = jnp.sum(jnp.where(lane == h, cq_all, 0.0), axis=1, keepdims=True)
        m_ref[...] = jnp.full(m_ref.shape, NEG_INF, F32)
        l_ref[...] = jnp.zeros_like(l_ref)
        acc_ref[...] = jnp.zeros_like(acc_ref)

        def body(j, carry):
            k0 = pl.multiple_of(j * tk, tk)
            s = _dot_nt(qh, kb_ref[pl.ds(k0, tk), :])
            ck = ck_ref[0, pl.ds(h, 1), pl.ds(k0, tk)]
            s = s + (cq - ck)
            s = jnp.where(k0 + kcol <= qrow, s, NEG_INF)
            m_old = m_ref[...]
            m_new = jnp.maximum(m_old, jnp.max(s, axis=1, keepdims=True))
            alpha = jnp.exp(m_old - m_new)
            pr = jnp.exp(s - m_new)
            l_ref[...] = alpha * l_ref[...] + jnp.sum(pr, axis=1, keepdims=True)
            acc_ref[...] = alpha * acc_ref[...] + jnp.dot(
                pr.astype(BF16), vb_ref[pl.ds(k0, tk), :], preferred_element_type=F32)
            m_ref[...] = m_new
            return carry

        n_kv = (i * tq + tq + tk - 1) // tk
        lax.fori_loop(0, n_kv, body, 0)
        outs.append(acc_ref[...] / l_ref[...])
    o_ref[...] = jnp.where(lane < HEAD_DIM, outs[0], outs[1])


def _fox_prompt(proj, c_rows, c_t, n_seq, seq_len, tq, tk, col_q, col_k, col_v):
    nq = seq_len // tq
    n_pairs = N_HEADS // 2
    kern = functools.partial(_fox_prompt_kernel, scale=HEAD_DIM ** -0.5, tk=tk)
    return pl.pallas_call(
        kern,
        grid=(n_seq, n_pairs, nq),
        in_specs=[pl.BlockSpec((tq, LANES), lambda b, p, i: (b * nq + i, col_q + p)),
                  pl.BlockSpec((seq_len, LANES), lambda b, p, i: (b, col_k + p)),
                  pl.BlockSpec((seq_len, LANES), lambda b, p, i: (b, col_v + p)),
                  pl.BlockSpec((tq, LANES), lambda b, p, i: (b * nq + i, 0)),
                  pl.BlockSpec((1, N_HEADS, seq_len), lambda b, p, i: (b, 0, 0))],
        out_specs=pl.BlockSpec((tq, LANES), lambda b, p, i: (b * nq + i, p)),
        out_shape=jax.ShapeDtypeStruct((n_seq * seq_len, n_pairs * LANES), F32),
        scratch_shapes=[pltpu.VMEM((seq_len, LANES), BF16), pltpu.VMEM((seq_len, LANES), BF16),
                        pltpu.VMEM((tq, 1), F32), pltpu.VMEM((tq, 1), F32),
                        pltpu.VMEM((tq, LANES), F32)],
        compiler_params=_params(("parallel", "parallel", "arbitrary")),
        name="fox_prompt",
    )(proj, proj, proj, c_rows, c_t)


def _lane_prefix(x):
    n = x.shape[1]
    lane = lax.broadcasted_iota(jnp.int32, x.shape, 1)
    s = 1
    while s < n:
        x = x + jnp.where(lane >= s, pltpu.roll(x, s, axis=1), 0.0)
        s *= 2
    return x


def _page_prefix_kernel(lf_ref, p_ref, tot_ref):
    lf = lf_ref[...]
    p_ref[...] = _lane_prefix(lf)
    tot_ref[...] = jnp.broadcast_to(jnp.sum(lf, axis=1, keepdims=True), lf.shape)


def _page_prefix(lf_rows, tr):
    rows, page = lf_rows.shape
    spec = pl.BlockSpec((tr, page), lambda i: (i, 0))
    shp = jax.ShapeDtypeStruct((rows, page), F32)
    return pl.pallas_call(
        _page_prefix_kernel,
        grid=(rows // tr,),
        in_specs=[spec], out_specs=[spec, spec], out_shape=[shp, shp],
        compiler_params=_params(("parallel",)),
        name="page_prefix",
    )(lf_rows)


def _fox_sample_kernel(pt_ref, q_ref, *refs, scale, n_new, n_grp):
    k_refs = refs[0:n_grp]
    v_refs = refs[n_grp:2 * n_grp]
    p_refs = refs[2 * n_grp:3 * n_grp]
    t_refs = refs[3 * n_grp:4 * n_grp]
    kn_ref, vn_ref, lfn_ref, o_ref, qb_ref, m_ref, l_ref, acc_ref, cb_ref = refs[4 * n_grp:]
    j = pl.program_id(1)
    n_steps = pl.num_programs(1)
    n_tok, c = q_ref.shape[1], q_ref.shape[2]
    rows = n_tok * N_HEADS
    row_i = lax.broadcasted_iota(jnp.int32, (N_HEADS, c), 0)
    lane_i = lax.broadcasted_iota(jnp.int32, (N_HEADS, c), 1)
    own_lanes = (lane_i >= row_i * HEAD_DIM) & (lane_i < (row_i + 1) * HEAD_DIM)

    @pl.when(j == 0)
    def _():
        q = q_ref[0] * scale
        for t in range(n_tok):
            qt = jnp.broadcast_to(q[t:t + 1, :], (N_HEADS, c))
            qb_ref[t * N_HEADS:(t + 1) * N_HEADS, :] = jnp.where(own_lanes, qt, 0.0).astype(BF16)
        m_ref[...] = jnp.full(m_ref.shape, NEG_INF, F32)
        l_ref[...] = jnp.zeros_like(l_ref)
        acc_ref[...] = jnp.zeros_like(acc_ref)
        cb_ref[...] = jnp.zeros_like(cb_ref)

    qb = qb_ref[...]

    def update(s_list, v_list):
        m_old = m_ref[...]
        m_new = m_old
        for s in s_list:
            m_new = jnp.maximum(m_new, jnp.max(s, axis=1, keepdims=True))
        alpha = jnp.exp(m_old - m_new)
        l_new = alpha * l_ref[...]
        acc = alpha * acc_ref[...]
        for s, vv in zip(s_list, v_list):
            pr = jnp.exp(s - m_new)
            l_new = l_new + jnp.sum(pr, axis=1, keepdims=True)
            acc = acc + _dot_nt(pr.astype(BF16), vv.astype(BF16))
        l_ref[...] = l_new
        acc_ref[...] = acc
        m_ref[...] = m_new

    base = cb_ref[...]
    s_list, v_list = [], []
    for g in range(n_grp):
        c_k = base + p_refs[g][0, 0]
        base = base + t_refs[g][0, 0]
        s = jnp.dot(qb, k_refs[g][0, 0].astype(BF16), preferred_element_type=F32)
        s_list.append(s - jnp.concatenate([c_k] * n_tok, axis=0))
        v_list.append(v_refs[g][0, 0])
    cb_ref[...] = base
    update(s_list, v_list)

    @pl.when(j == n_steps - 1)
    def _():
        nk_new = kn_ref.shape[2]
        c_new = cb_ref[:, 0:nk_new] + _lane_prefix(lfn_ref[0])
        s = jnp.dot(qb, kn_ref[0].astype(BF16), preferred_element_type=F32)
        s = s - jnp.concatenate([c_new] * n_tok, axis=0)
        r_n = lax.broadcasted_iota(jnp.int32, (rows, nk_new), 0)
        c_n = lax.broadcasted_iota(jnp.int32, (rows, nk_new), 1)
        ok = (c_n * N_HEADS <= (r_n | (N_HEADS - 1))) & (c_n < n_new)
        update([jnp.where(ok, s, NEG_INF)], [vn_ref[0]])
        out = acc_ref[...] / l_ref[...]
        for t in range(n_tok):
            blk = out[t * N_HEADS:(t + 1) * N_HEADS, :]
            o_ref[0, t:t + 1, :] = jnp.sum(jnp.where(own_lanes, blk, 0.0), axis=0, keepdims=True)


def _fox_sample(page_table, layer, q, k_t, v_t, pfx, tot, k_new, v_new, lf_new, n_new, n_grp):
    db, dt, c = q.shape
    n_pages = page_table.shape[1]
    page = k_t.shape[3]
    nkn = k_new.shape[2]
    rows = dt * N_HEADS
    kern = functools.partial(_fox_sample_kernel, scale=HEAD_DIM ** -0.5, n_new=n_new, n_grp=n_grp)

    def paged(shape2, g):
        return pl.BlockSpec((1, 1) + shape2,
                            lambda b, j, pt: (layer, pt[b, j * n_grp + g], 0, 0))

    per_b = lambda shape2: pl.BlockSpec((1,) + shape2, lambda b, j, pt: (b, 0, 0))
    grid_spec = pltpu.PrefetchScalarGridSpec(
        num_scalar_prefetch=1,
        grid=(db, n_pages // n_grp),
        in_specs=[per_b((dt, c))]
        + [paged((c, page), g) for g in range(n_grp)]
        + [paged((c, page), g) for g in range(n_grp)]
        + [paged((N_HEADS, page), g) for g in range(n_grp)]
        + [paged((N_HEADS, page), g) for g in range(n_grp)]
        + [per_b((c, nkn)), per_b((c, nkn)), per_b((N_HEADS, nkn))],
        out_specs=per_b((dt, c)),
        scratch_shapes=[pltpu.VMEM((rows, c), BF16), pltpu.VMEM((rows, 1), F32),
                        pltpu.VMEM((rows, 1), F32), pltpu.VMEM((rows, c), F32),
                        pltpu.VMEM((N_HEADS, page), F32)],
    )
    return pl.pallas_call(
        kern,
        grid_spec=grid_spec,
        out_shape=jax.ShapeDtypeStruct((db, dt, c), F32),
        compiler_params=_params(("parallel", "arbitrary")),
        name="fox_sample",
    )(page_table, q, *([k_t] * n_grp), *([v_t] * n_grp), *([pfx] * n_grp), *([tot] * n_grp),
      k_new, v_new, lf_new)


def _sample_mix_kernel(xa_ref, sconv_ref, h0_ref, xc_ref, spool_ref, du_ref, dv_ref,
                       cw_ref, cb_ref, wa_ref, ba_ref, wx_ref, bx_ref, lam_ref,
                       pw_ref, psc_ref, lng_ref, lnb_ref, sw_ref, sb_ref,
                       oa_ref, oc_ref, vn_ref, od_ref, *, pos0):
    nt = xa_ref.shape[0]
    cw = cw_ref[...]
    ext = [sconv_ref[j] for j in range(CONV_W - 1)] + [xa_ref[t] for t in range(nt)]
    h = h0_ref[...]
    for t in range(nt):
        xc = cb_ref[...]
        for j in range(CONV_W):
            xc = xc + cw[j:j + 1, :] * ext[t + j]
        a, u = _lru_gates(xc, wa_ref[...], ba_ref[...], wx_ref[...], bx_ref[...], lam_ref[...])
        h = a * h + u
        oa_ref[t] = h
    pext = [spool_ref[j] for j in range(POOL_HIST)] + [xc_ref[t] for t in range(nt)]
    for t in range(nt):
        x_t = pext[POOL_HIST + t]
        for g, win in enumerate(POOL_WINDOWS):
            lo, hi = g * GROUP_W, (g + 1) * GROUP_W
            wsum = x_t[:, lo:hi]
            for j in range(1, win):
                wsum = wsum + pext[POOL_HIST + t - j][:, lo:hi]
            cnt = float(min(win, pos0 + t + 1))
            pooled = wsum / cnt - x_t[:, lo:hi]
            y = jnp.dot(pooled.astype(BF16), pw_ref[g], preferred_element_type=F32)
            oc_ref[t, :, lo:hi] = y * psc_ref[:, lo:hi]
    vns = []
    for t in range(nt):
        vn = _layer_norm(_gelu(dv_ref[t]), lng_ref[...], lnb_ref[...])
        vn_ref[t] = vn
        vns.append(vn)
    for t in range(nt):
        mixed = sb_ref[t]
        for s in range(t + 1):
            mixed = mixed + sw_ref[t, s] * vns[s]
        od_ref[t] = _gelu(du_ref[t]) * mixed


def _sample_mix(xa, sconv, h0, xc, spool, du, dv, cw, cb, wa, ba, wx, bx, lam,
                pw, psc, lng, lnb, sw, sb, pos0):
    shp = jax.ShapeDtypeStruct(xa.shape, F32)
    kern = functools.partial(_sample_mix_kernel, pos0=pos0)
    return pl.pallas_call(
        kern,
        out_shape=[shp, shp, shp, shp],
        compiler_params=pltpu.CompilerParams(vmem_limit_bytes=VMEM_LIMIT_V7X),
        name="sample_mix",
    )(xa, sconv, h0, xc, spool, du, dv, cw, cb, wa, ba, wx, bx, lam, pw, psc, lng, lnb, sw, sb)


def _merge_kernel(x_ref, oa_ref, ob_ref, oc_ref, od_ref, g0_ref, g1_ref, g2_ref, g3_ref,
                  wb_ref, o_ref):
    x = x_ref[...]
    acc = None
    for n, (o_n, g_n) in enumerate(((oa_ref, g0_ref), (ob_ref, g1_ref),
                                    (oc_ref, g2_ref), (od_ref, g3_ref))):
        gate = jax.nn.sigmoid(jnp.dot(x, g_n[0], preferred_element_type=F32))
        br = jnp.dot(o_n[...].astype(BF16), wb_ref[0, n], preferred_element_type=F32)
        acc = gate * br if acc is None else acc + gate * br
    o_ref[...] = acc.astype(o_ref.dtype)


def _merge(xb, oa, ob, oc, od, wg_all, gate_col0, wb_all, layer, tm, tn):
    m, d = xb.shape
    c = oa.shape[1]
    ncol = d // tn
    blk0 = gate_col0 // tn
    o_spec = pl.BlockSpec((tm, c), lambda i, j: (i, 0))
    g_specs = [pl.BlockSpec((1, d, tn), (lambda n: (lambda i, j: (layer, 0, blk0 + n * ncol + j)))(n))
               for n in range(N_BRANCH)]
    return pl.pallas_call(
        _merge_kernel,
        grid=(m // tm, ncol),
        in_specs=[pl.BlockSpec((tm, d), lambda i, j: (i, 0)), o_spec, o_spec, o_spec, o_spec]
        + g_specs + [pl.BlockSpec((1, N_BRANCH, c, tn), lambda i, j: (layer, 0, 0, j))],
        out_specs=pl.BlockSpec((tm, tn), lambda i, j: (i, j)),
        out_shape=jax.ShapeDtypeStruct((m, d), BF16),
        compiler_params=_params(("parallel", "arbitrary")),
        name="gated_merge",
    )(xb, oa, ob, oc, od, wg_all, wg_all, wg_all, wg_all, wb_all)


def _out_ln_kernel(m_ref, w_ref, x_ref, g_ref, b_ref, o_ref, ob_ref, *, alpha):
    y = jnp.dot(m_ref[...], w_ref[0], preferred_element_type=F32)
    out = _layer_norm(alpha * x_ref[...] + y, g_ref[...], b_ref[...])
    o_ref[...] = out
    ob_ref[...] = out.astype(BF16)


def _out_ln(merged, w_all, layer, x, g, b, alpha, tm):
    m, d = x.shape
    row = pl.BlockSpec((tm, d), lambda i: (i, 0))
    vec = pl.BlockSpec((1, d), lambda i: (0, 0))
    return pl.pallas_call(
        functools.partial(_out_ln_kernel, alpha=alpha),
        grid=(m // tm,),
        in_specs=[row, pl.BlockSpec((1, d, d), lambda i: (layer, 0, 0)), row, vec, vec],
        out_specs=[row, row],
        out_shape=[jax.ShapeDtypeStruct((m, d), F32), jax.ShapeDtypeStruct((m, d), BF16)],
        compiler_params=_params(("parallel",)),
        name="out_proj_ln",
    )(merged, w_all, x, g, b)


def _add_ln_kernel(y_ref, x_ref, g_ref, b_ref, o_ref, ob_ref, *, alpha):
    out = _layer_norm(alpha * x_ref[...] + y_ref[...], g_ref[...], b_ref[...])
    o_ref[...] = out
    ob_ref[...] = out.astype(BF16)


def _add_ln(y, x, g, b, alpha, tm):
    m, d = x.shape
    row = pl.BlockSpec((tm, d), lambda i: (i, 0))
    vec = pl.BlockSpec((1, d), lambda i: (0, 0))
    return pl.pallas_call(
        functools.partial(_add_ln_kernel, alpha=alpha),
        grid=(m // tm,),
        in_specs=[row, row, vec, vec],
        out_specs=[row, row],
        out_shape=[jax.ShapeDtypeStruct((m, d), F32), jax.ShapeDtypeStruct((m, d), BF16)],
        compiler_params=_params(("parallel",)),
        name="peer_add_ln",
    )(y, x, g, b)


def _top_rows(s, k):
    vals = []
    for _ in range(k):
        mx = jnp.max(s, axis=0, keepdims=True)
        vals.append(mx)
        s = jnp.where(s == mx, NEG_INF, s)
    return vals


def _peer_score_kernel(x_ref, wq_ref, k1_ref, k2_ref, s1_ref, s2_ref, st_ref):
    half = k1_ref.shape[1]
    tm = x_ref.shape[0]
    q = jnp.dot(x_ref[...], wq_ref[0], preferred_element_type=F32)
    s1 = _dot_nt(k1_ref[...], q[:, :half].astype(BF16))
    s2 = _dot_nt(k2_ref[...], q[:, half:].astype(BF16))
    s1_ref[0] = s1
    s2_ref[0] = s2
    for c0 in range(0, tm, LANES):
        v1 = _top_rows(s1[:, c0:c0 + LANES], PEER_TOPK)
        v2 = _top_rows(s2[:, c0:c0 + LANES], PEER_TOPK)
        v2cat = jnp.concatenate(v2, axis=0)
        cand = jnp.concatenate([v1[a] + v2cat for a in range(PEER_TOPK)], axis=0)
        top = _top_rows(cand, PEER_TOPK)
        m = top[0]
        z = jnp.exp(top[0] - m)
        for kk in range(1, PEER_TOPK):
            z = z + jnp.exp(top[kk] - m)
        zero = jnp.zeros_like(m)
        st_ref[0, :, c0:c0 + LANES] = jnp.concatenate(
            [top[PEER_TOPK - 1], v1[0], v2[0], 1.0 / z, zero, zero, zero, zero], axis=0)


def _peer_scores(xb, wq_all, layer, k1, k2, tm):
    n, d = xb.shape
    nk, half = k1.shape
    big = pl.BlockSpec((1, nk, tm), lambda i, h: (h, 0, i))
    return pl.pallas_call(
        _peer_score_kernel,
        grid=(n // tm, N_HEADS),
        in_specs=[pl.BlockSpec((tm, d), lambda i, h: (i, 0)),
                  pl.BlockSpec((1, d, 2 * half), lambda i, h: (layer, 0, h)),
                  pl.BlockSpec(k1.shape, lambda i, h: (0, 0)),
                  pl.BlockSpec(k2.shape, lambda i, h: (0, 0))],
        out_specs=[big, big, pl.BlockSpec((1, 8, tm), lambda i, h: (h, 0, i))],
        out_shape=[jax.ShapeDtypeStruct((N_HEADS, nk, n), F32),
                   jax.ShapeDtypeStruct((N_HEADS, nk, n), F32),
                   jax.ShapeDtypeStruct((N_HEADS, 8, n), F32)],
        compiler_params=_params(("parallel", "arbitrary")),
        name="peer_scores",
    )(xb, wq_all, k1, k2)


def _peer_dense_kernel(x_ref, u_ref, vt_ref, s1_ref, s2_ref, st_ref, y_ref,
                       acc_ref, e1_ref, e2_ref, g_ref):
    j = pl.program_id(1)
    n_e = pl.num_programs(1)
    te = u_ref.shape[1]
    nk = s2_ref.shape[1]

    @pl.when(j == 0)
    def _():
        acc_ref[...] = jnp.zeros_like(acc_ref)
        for h in range(N_HEADS):
            e1_ref[h] = jnp.exp(s1_ref[h] - st_ref[h, 1:2, :])
            e2_ref[h] = jnp.exp(s2_ref[h] - st_ref[h, 2:3, :]) * st_ref[h, 3:4, :]

    act = _gelu(_dot_nt(u_ref[0], x_ref[...]))
    for r in range(te // nk):
        i1 = j * (te // nk) + r
        w = None
        for h in range(N_HEADS):
            tot = s1_ref[h, pl.ds(i1, 1), :] + s2_ref[h]
            term = jnp.where(tot >= st_ref[h, 0:1, :],
                             e1_ref[h, pl.ds(i1, 1), :] * e2_ref[h], 0.0)
            w = term if w is None else w + term
        g_ref[r * nk:(r + 1) * nk, :] = (act[r * nk:(r + 1) * nk, :] * w).astype(BF16)
    acc_ref[...] += jnp.dot(vt_ref[0], g_ref[...], preferred_element_type=F32)

    @pl.when(j == n_e - 1)
    def _():
        y_ref[...] = acc_ref[...].T


def _peer_dense(xb, u_all, vt_all, layer, s1t, s2t, st, tm, te):
    n_rows, d = xb.shape
    n_exp = u_all.shape[1]
    nk = s1t.shape[1]
    tok3 = lambda i, j: (0, 0, i)
    return pl.pallas_call(
        _peer_dense_kernel,
        grid=(n_rows // tm, n_exp // te),
        in_specs=[pl.BlockSpec((tm, d), lambda i, j: (i, 0)),
                  pl.BlockSpec((1, te, d), lambda i, j: (layer, j, 0)),
                  pl.BlockSpec((1, d, te), lambda i, j: (layer, 0, j)),
                  pl.BlockSpec((N_HEADS, nk, tm), tok3),
                  pl.BlockSpec((N_HEADS, nk, tm), tok3),
                  pl.BlockSpec((N_HEADS, 8, tm), tok3)],
        out_specs=pl.BlockSpec((tm, d), lambda i, j: (i, 0)),
        out_shape=jax.ShapeDtypeStruct((n_rows, d), F32),
        scratch_shapes=[pltpu.VMEM((d, tm), F32),
                        pltpu.VMEM((N_HEADS, nk, tm), F32), pltpu.VMEM((N_HEADS, nk, tm), F32),
                        pltpu.VMEM((te, tm), BF16)],
        compiler_params=_params(("parallel", "arbitrary")),
        name="peer_dense",
    )(xb, u_all, vt_all, s1t, s2t, st)


def _largest_divisor(n, cap, mult):
    best = None
    for t in range(mult, cap + 1, mult):
        if n % t == 0:
            best = t
    assert best is not None, (n, cap, mult)
    return best


def _block_diag(w):
    h, a, b = w.shape
    eye = jnp.eye(h, dtype=w.dtype)
    return (eye[:, None, :, None] * w[:, :, None, :]).reshape(h * a, h * b)


def _channel_mix(l, depth, x, xb, mixed, w, tm_merge, tm, te):
    alpha = (2 * depth) ** 0.25
    n = x.shape[0]
    merged = _merge(xb, *mixed, w["w_tail"], w["gate_col0"], w["wb"], l, tm_merge, 256)
    x1, x1b = _out_ln(merged, w["wo"], l, x, w["ln1_g"][l], w["ln1_b"][l], alpha,
                      _largest_divisor(n, 256, 16))
    s1t, s2t, st = _peer_scores(x1b, w["wq"], l, w["k1"][l], w["k2"][l], tm)
    y = _peer_dense(x1b, w["u"], w["vt"], l, s1t, s2t, st, tm, te)
    return _add_ln(y, x1, w["ln2_g"][l], w["ln2_b"][l], alpha, tm)


def kernel(x_prompt, x_sample, cache_k, cache_v, cache_logf, state_conv, state_lru, state_pool, page_table, w_in, conv_w, conv_b, lru_wa, lru_ba, lru_wx, lru_bx, lru_lambda, fox_bf, pool_w, pool_scale, sgu_ln_g, sgu_ln_b, sgu_w, sgu_b, w_branch, w_out, ln1_g, ln1_b, peer_wq, peer_k1, peer_k2, peer_u, peer_v, ln2_g, ln2_b):
    n_seq, seq_len, d = x_prompt.shape
    db, dt, _ = x_sample.shape
    depth = w_in.shape[0]
    c = d // N_BRANCH
    n_p, n_s = n_seq * seq_len, db * dt
    n_pool, page = cache_k.shape[1], cache_k.shape[2]
    n_pages = page_table.shape[1]
    past_len = n_pages * page
    assert c == N_HEADS * HEAD_DIM == N_BRANCH * GROUP_W
    assert seq_len % SGU_CHUNK == 0 and dt <= SGU_CHUNK and dt <= page and page == LANES

    fcol = 4 * c
    rows = lambda v: v.reshape(depth, 1, -1)
    w = dict(
        w_tail=w_in[:, :, fcol + N_HEADS:].astype(BF16),
        gate_col0=3 * c,
        wb=w_branch.astype(BF16), wo=w_out.astype(BF16), wq=peer_wq.astype(BF16),
        k1=peer_k1.astype(BF16), k2=peer_k2.astype(BF16),
        u=peer_u.astype(BF16), vt=peer_v.transpose(0, 2, 1).astype(BF16),
        ln1_g=rows(ln1_g), ln1_b=rows(ln1_b), ln2_g=rows(ln2_g), ln2_b=rows(ln2_b))
    w_head = w_in[:, :, :fcol].astype(BF16)
    w_f = jnp.pad(w_in[:, :, fcol:fcol + N_HEADS], ((0, 0), (0, 0), (0, LANES - N_HEADS))).astype(BF16)
    b_f = jnp.pad(fox_bf, ((0, 0), (0, LANES - N_HEADS))).reshape(depth, 1, LANES)
    wa = jax.vmap(_block_diag)(lru_wa).astype(BF16)
    wx = jax.vmap(_block_diag)(lru_wx).astype(BF16)
    cb, ba, bx, lam = rows(conv_b), rows(lru_ba), rows(lru_bx), rows(lru_lambda)
    pw = pool_w.astype(BF16)
    psc, lng, lnb = rows(pool_scale), rows(sgu_ln_g), rows(sgu_ln_b)

    k_t = cache_k.transpose(0, 1, 3, 4, 2).reshape(depth, n_pool, c, page)
    v_t = cache_v.transpose(0, 1, 3, 4, 2).reshape(depth, n_pool, c, page)
    lf_rows = cache_logf.transpose(0, 1, 3, 2).reshape(depth * n_pool * N_HEADS, page)
    pfx, tot = _page_prefix(lf_rows, _largest_divisor(lf_rows.shape[0], 2048, 8))
    pfx = pfx.reshape(depth, n_pool, N_HEADS, page)
    tot = tot.reshape(depth, n_pool, N_HEADS, page)

    tm_p = _largest_divisor(n_p, 1024, 16)
    tt = _largest_divisor(seq_len, 512, 8)
    te = 512
    n_grp = _largest_divisor(n_pages, 8, 1)
    hd = (N_HEADS, HEAD_DIM)
    tmaj = lambda a: a.reshape(db, dt, -1).transpose(1, 0, 2)
    bmaj = lambda a: a.transpose(1, 0, 2).reshape(n_s, -1)
    pad_lanes = lambda a: jnp.pad(a, ((0, 0), (0, 0), (0, LANES - a.shape[2])))

    xp, xs = x_prompt.reshape(n_p, d), x_sample.reshape(n_s, d)
    xpb, xsb = xp.astype(BF16), xs.astype(BF16)
    p_states, s_states = [], []
    for l in range(depth):
        ph = _matmul(xpb, w_head, l, fcol, tm_p, 512)
        pt = _matmul(xpb, w["w_tail"], l, 3 * c, tm_p, 512)
        sh = _matmul(xsb, w_head, l, fcol, n_s, 512)
        st = _matmul(xsb, w["w_tail"], l, 3 * c, n_s, 512)
        lf_p, c_p = _logf(xpb, w_f[l], b_f[l], n_seq, seq_len, tt)
        lf_s, _ = _logf(xsb, w_f[l], b_f[l], 1, n_s, n_s)
        c_t = c_p[:, :N_HEADS].reshape(n_seq, seq_len, N_HEADS).transpose(0, 2, 1)

        oa_p = _lru_prompt(ph, 0, n_seq, seq_len, tt, conv_w[l], cb[l], wa[l], ba[l], wx[l], bx[l],
                           lam[l])
        oc_p = _pool_prompt(pt, 0, n_seq, seq_len, tt, pw[l], psc[l])
        od_p = _sgu_prompt(pt, 1, 2, n_p, tt, lng[l], lnb[l], sgu_w[l], sgu_b[l].T)
        ob_p = _fox_prompt(ph, c_p, c_t, n_seq, seq_len, tt, tt,
                           c // LANES, 2 * c // LANES, 3 * c // LANES)

        sw = jnp.repeat(sgu_w[l, :, :dt, :dt].transpose(1, 2, 0), GROUP_W, axis=2).reshape(dt, dt, 1, c)
        sb = jnp.repeat(sgu_b[l, :, :dt].T, GROUP_W, axis=1).reshape(dt, 1, c)
        oa_s, oc_s, vn_s, od_s = _sample_mix(
            tmaj(sh[:, :c]), state_conv[l].transpose(1, 0, 2), state_lru[l],
            tmaj(st[:, :c]), state_pool[l].transpose(1, 0, 2),
            tmaj(st[:, c:2 * c]), tmaj(st[:, 2 * c:]),
            conv_w[l], cb[l], wa[l], ba[l], wx[l], bx[l], lam[l], pw[l], psc[l], lng[l], lnb[l],
            sw, sb, past_len)
        new_t = lambda a: pad_lanes(a.reshape(db, dt, -1).transpose(0, 2, 1))
        ob_s = _fox_sample(page_table, l, sh[:, c:2 * c].reshape(db, dt, c), k_t, v_t, pfx, tot,
                           new_t(sh[:, 2 * c:3 * c]), new_t(sh[:, 3 * c:]),
                           new_t(lf_s[:, :N_HEADS]), dt, n_grp).reshape(n_s, c)

        p3 = lambda a: a.reshape(n_seq, seq_len, -1)
        p_states.append((
            ph[:, 2 * c:3 * c].reshape(n_seq, seq_len, *hd),
            ph[:, 3 * c:].reshape(n_seq, seq_len, *hd),
            lf_p[:, :N_HEADS].reshape(n_seq, seq_len, N_HEADS),
            p3(ph)[:, seq_len - (CONV_W - 1):, :c],
            p3(oa_p)[:, -1],
            p3(pt)[:, seq_len - POOL_HIST:, :c]))
        s3 = lambda a: a.reshape(db, dt, -1)
        s_states.append((
            sh[:, 2 * c:3 * c].reshape(db, dt, *hd),
            sh[:, 3 * c:].reshape(db, dt, *hd),
            lf_s[:, :N_HEADS].reshape(db, dt, N_HEADS),
            jnp.concatenate([state_conv[l], s3(sh)[:, :, :c]], axis=1)[:, -(CONV_W - 1):],
            oa_s[dt - 1],
            jnp.concatenate([state_pool[l], s3(st)[:, :, :c]], axis=1)[:, -POOL_HIST:],
            vn_s.transpose(1, 0, 2)))

        xp, xpb = _channel_mix(l, depth, xp, xpb, (oa_p, ob_p, oc_p, od_p), w, tm_p, 512, te)
        xs, xsb = _channel_mix(l, depth, xs, xsb,
                               (bmaj(oa_s), ob_s, bmaj(oc_s), bmaj(od_s)), w, n_s, n_s, te)

    yp = xp.reshape(n_seq, seq_len, d)
    ys = xs.reshape(db, dt, d)
    p_out = [jnp.stack([s[j] for s in p_states]) for j in range(6)]
    s_out = [jnp.stack([s[j] for s in s_states]) for j in range(7)]
    return (yp, ys, *p_out, *s_out)
```

```python
import functools

import jax
import jax.numpy as jnp
from jax import lax
from jax.experimental import pallas as pl
from jax.experimental.pallas import tpu as pltpu

F32 = jnp.float32
BF16 = jnp.bfloat16

N_BRANCH = 4
N_HEADS = 8
HEAD_DIM = 64
CONV_W = 4
LRU_C = 8.0
POOL_WINDOWS = (2, 4, 8, 16)
POOL_HIST = 15
GROUP_W = 128
SGU_CHUNK = 128
PEER_NKEYS = 128
PEER_TOPK = 16
LN_EPS = 1e-5
NEG_INF = float("-inf")

VMEM_LIMIT_V7X = 56 * 1024 * 1024
LANES = 128


def _params(semantics):
    return pltpu.CompilerParams(dimension_semantics=semantics, vmem_limit_bytes=VMEM_LIMIT_V7X)


def _gelu(x):
    return 0.5 * x * (1.0 + lax.erf(x * 0.7071067811865476))


def _softplus(z):
    return jnp.maximum(z, 0.0) + jnp.log1p(jnp.exp(-jnp.abs(z)))


def _log_sigmoid(z):
    return -_softplus(-z)


def _expm1(y):
    e = jnp.exp(y)
    return jnp.where(e == 1.0, y, (e - 1.0) * y / jnp.log(e))


def _layer_norm(x, g, b):
    mu = jnp.mean(x, axis=-1, keepdims=True)
    d = x - mu
    var = jnp.mean(d * d, axis=-1, keepdims=True)
    return d * lax.rsqrt(var + LN_EPS) * g + b


def _cumsum_rows(x):
    n = x.shape[0]
    rows = lax.broadcasted_iota(jnp.int32, x.shape, 0)
    s = 1
    while s < n:
        x = x + jnp.where(rows >= s, pltpu.roll(x, s, axis=0), 0.0)
        s *= 2
    return x


def _dot_nt(a, b):
    return lax.dot_general(a, b, (((1,), (1,)), ((), ())), preferred_element_type=F32)


def _matmul_kernel(x_ref, w_ref, o_ref):
    o_ref[...] = jnp.dot(x_ref[...], w_ref[0], preferred_element_type=F32)


def _matmul(xb, w_all, layer, n_cols, tm, tn):
    m, k = xb.shape
    return pl.pallas_call(
        _matmul_kernel,
        grid=(m // tm, n_cols // tn),
        in_specs=[pl.BlockSpec((tm, k), lambda i, j: (i, 0)),
                  pl.BlockSpec((1, k, tn), lambda i, j: (layer, 0, j))],
        out_specs=pl.BlockSpec((tm, tn), lambda i, j: (i, j)),
        out_shape=jax.ShapeDtypeStruct((m, n_cols), F32),
        compiler_params=_params(("parallel", "arbitrary")),
        name="in_proj",
    )(xb, w_all)


def _logf_kernel(x_ref, w_ref, b_ref, lf_ref, c_ref, carry_ref):
    i = pl.program_id(1)
    fl = jnp.dot(x_ref[...], w_ref[...], preferred_element_type=F32) + b_ref[...]
    lf = _log_sigmoid(fl)
    lf_ref[...] = lf

    @pl.when(i == 0)
    def _():
        carry_ref[...] = jnp.zeros_like(carry_ref)

    c = _cumsum_rows(lf) + carry_ref[...]
    c_ref[...] = c
    carry_ref[...] = c[c.shape[0] - 1:, :]


def _logf(xb, w_f, b_f, n_seq, seq_len, tt):
    k = xb.shape[1]
    nt = seq_len // tt
    rows = n_seq * seq_len
    return pl.pallas_call(
        _logf_kernel,
        grid=(n_seq, nt),
        in_specs=[pl.BlockSpec((tt, k), lambda b, i: (b * nt + i, 0)),
                  pl.BlockSpec((k, LANES), lambda b, i: (0, 0)),
                  pl.BlockSpec((1, LANES), lambda b, i: (0, 0))],
        out_specs=[pl.BlockSpec((tt, LANES), lambda b, i: (b * nt + i, 0)),
                   pl.BlockSpec((tt, LANES), lambda b, i: (b * nt + i, 0))],
        out_shape=[jax.ShapeDtypeStruct((rows, LANES), F32),
                   jax.ShapeDtypeStruct((rows, LANES), F32)],
        scratch_shapes=[pltpu.VMEM((1, LANES), F32)],
        compiler_params=_params(("parallel", "arbitrary")),
        name="log_forget",
    )(xb, w_f, b_f)


HALO = 8


def _lru_gates(xc, wa, ba, wx, bx, lam):
    xcb = xc.astype(BF16)
    r = jax.nn.sigmoid(jnp.dot(xcb, wa, preferred_element_type=F32) + ba)
    ig = jax.nn.sigmoid(jnp.dot(xcb, wx, preferred_element_type=F32) + bx)
    log_a = -LRU_C * r * _softplus(-lam)
    a = jnp.exp(log_a)
    u = jnp.sqrt(-_expm1(2.0 * log_a)) * (ig * xc)
    return a, u


def _lru_prompt_kernel(xa_ref, cw_ref, cb_ref, wa_ref, ba_ref, wx_ref, bx_ref, lam_ref,
                       o_ref, ext_ref, h_ref, a_ref, u_ref):
    i = pl.program_id(1)
    tt = xa_ref.shape[0]

    @pl.when(i == 0)
    def _():
        ext_ref[0:HALO, :] = jnp.zeros((HALO, ext_ref.shape[1]), F32)
        h_ref[...] = jnp.zeros_like(h_ref)

    @pl.when(i > 0)
    def _():
        ext_ref[0:HALO, :] = ext_ref[tt:tt + HALO, :]

    ext_ref[HALO:HALO + tt, :] = xa_ref[...]
    cw = cw_ref[...]
    xc = cb_ref[...]
    for j in range(CONV_W):
        off = HALO - (CONV_W - 1) + j
        xc = xc + cw[j:j + 1, :] * ext_ref[off:off + tt, :]
    a, u = _lru_gates(xc, wa_ref[...], ba_ref[...], wx_ref[...], bx_ref[...], lam_ref[...])
    a_ref[...] = a
    u_ref[...] = u

    def body(t, h):
        h = a_ref[pl.ds(t, 1), :] * h + u_ref[pl.ds(t, 1), :]
        o_ref[pl.ds(t, 1), :] = h
        return h

    h_ref[...] = lax.fori_loop(0, tt, body, h_ref[...])


def _lru_prompt(proj, col_blk, n_seq, seq_len, tt, cw, cb, wa, ba, wx, bx, lam):
    c = cw.shape[1]
    nt = seq_len // tt
    const = lambda b, i: (0, 0)
    return pl.pallas_call(
        _lru_prompt_kernel,
        grid=(n_seq, nt),
        in_specs=[pl.BlockSpec((tt, c), lambda b, i: (b * nt + i, col_blk)),
                  pl.BlockSpec((CONV_W, c), const), pl.BlockSpec((1, c), const),
                  pl.BlockSpec((c, c), const), pl.BlockSpec((1, c), const),
                  pl.BlockSpec((c, c), const), pl.BlockSpec((1, c), const),
                  pl.BlockSpec((1, c), const)],
        out_specs=pl.BlockSpec((tt, c), lambda b, i: (b * nt + i, 0)),
        out_shape=jax.ShapeDtypeStruct((n_seq * seq_len, c), F32),
        scratch_shapes=[pltpu.VMEM((HALO + tt, c), F32), pltpu.VMEM((1, c), F32),
                        pltpu.VMEM((tt, c), F32), pltpu.VMEM((tt, c), F32)],
        compiler_params=_params(("parallel", "arbitrary")),
        name="lru_prompt",
    )(proj, cw, cb, wa, ba, wx, bx, lam)


POOL_HALO = 16


def _pool_prompt_kernel(x_ref, w_ref, sc_ref, o_ref, ext_ref):
    i = pl.program_id(1)
    tt = x_ref.shape[0]

    @pl.when(i == 0)
    def _():
        ext_ref[0:POOL_HALO, :] = jnp.zeros((POOL_HALO, ext_ref.shape[1]), F32)

    @pl.when(i > 0)
    def _():
        ext_ref[0:POOL_HALO, :] = ext_ref[tt:tt + POOL_HALO, :]

    ext_ref[POOL_HALO:POOL_HALO + tt, :] = x_ref[...]
    pos = i * tt + lax.broadcasted_iota(jnp.int32, (tt, GROUP_W), 0)
    for g, win in enumerate(POOL_WINDOWS):
        lo, hi = g * GROUP_W, (g + 1) * GROUP_W
        wsum = ext_ref[POOL_HALO:POOL_HALO + tt, lo:hi]
        for j in range(1, win):
            wsum = wsum + ext_ref[POOL_HALO - j:POOL_HALO - j + tt, lo:hi]
        cnt = jnp.minimum(win, pos + 1).astype(F32)
        pooled = wsum / cnt - x_ref[:, lo:hi]
        y = jnp.dot(pooled.astype(BF16), w_ref[g], preferred_element_type=F32)
        o_ref[:, lo:hi] = y * sc_ref[:, lo:hi]


def _pool_prompt(proj, col_blk, n_seq, seq_len, tt, w, scale):
    c = scale.shape[1]
    nt = seq_len // tt
    return pl.pallas_call(
        _pool_prompt_kernel,
        grid=(n_seq, nt),
        in_specs=[pl.BlockSpec((tt, c), lambda b, i: (b * nt + i, col_blk)),
                  pl.BlockSpec(w.shape, lambda b, i: (0, 0, 0)),
                  pl.BlockSpec((1, c), lambda b, i: (0, 0))],
        out_specs=pl.BlockSpec((tt, c), lambda b, i: (b * nt + i, 0)),
        out_shape=jax.ShapeDtypeStruct((n_seq * seq_len, c), F32),
        scratch_shapes=[pltpu.VMEM((POOL_HALO + tt, c), F32)],
        compiler_params=_params(("parallel", "arbitrary")),
        name="pool_prompt",
    )(proj, w, scale)


def _sgu_prompt_kernel(du_ref, dv_ref, g_ref, b_ref, w_ref, sbt_ref, o_ref):
    tt = du_ref.shape[0]
    vn = _layer_norm(_gelu(dv_ref[...]), g_ref[...], b_ref[...])
    vnb = vn.astype(BF16)
    gu = _gelu(du_ref[...])
    row = lax.broadcasted_iota(jnp.int32, (SGU_CHUNK, SGU_CHUNK), 0)
    col = lax.broadcasted_iota(jnp.int32, (SGU_CHUNK, SGU_CHUNK), 1)
    sbt = sbt_ref[...]
    for g in range(N_BRANCH):
        lo, hi = g * GROUP_W, (g + 1) * GROUP_W
        wt = jnp.where(col <= row, w_ref[g], 0.0).astype(BF16)
        bcol = sbt[:, g:g + 1]
        for ch in range(tt // SGU_CHUNK):
            r0, r1 = ch * SGU_CHUNK, (ch + 1) * SGU_CHUNK
            mixed = jnp.dot(wt, vnb[r0:r1, lo:hi], preferred_element_type=F32) + bcol
            o_ref[r0:r1, lo:hi] = gu[r0:r1, lo:hi] * mixed


def _sgu_prompt(proj, col_u, col_v, n_rows, tt, ln_g, ln_b, w, sbt):
    c = ln_g.shape[1]
    return pl.pallas_call(
        _sgu_prompt_kernel,
        grid=(n_rows // tt,),
        in_specs=[pl.BlockSpec((tt, c), lambda i: (i, col_u)),
                  pl.BlockSpec((tt, c), lambda i: (i, col_v)),
                  pl.BlockSpec((1, c), lambda i: (0, 0)),
                  pl.BlockSpec((1, c), lambda i: (0, 0)),
                  pl.BlockSpec(w.shape, lambda i: (0, 0, 0)),
                  pl.BlockSpec(sbt.shape, lambda i: (0, 0))],
        out_specs=pl.BlockSpec((tt, c), lambda i: (i, 0)),
        out_shape=jax.ShapeDtypeStruct((n_rows, c), F32),
        compiler_params=_params(("parallel",)),
        name="sgu_prompt",
    )(proj, proj, ln_g, ln_b, w, sbt)


def _fox_prompt_kernel(q_ref, k_ref, v_ref, cq_ref, ck_ref, o_ref, kb_ref, vb_ref,
                       m_ref, l_ref, acc_ref, *, scale, tk):
    p = pl.program_id(1)
    i = pl.program_id(2)
    tq = q_ref.shape[0]

    @pl.when(i == 0)
    def _():
        kb_ref[...] = k_ref[...].astype(BF16)
        vb_ref[...] = v_ref[...].astype(BF16)

    lane = lax.broadcasted_iota(jnp.int32, (tq, LANES), 1)
    q = q_ref[...] * scale
    cq_all = cq_ref[...]
    qrow = i * tq + lax.broadcasted_iota(jnp.int32, (tq, tk), 0)
    kcol = lax.broadcasted_iota(jnp.int32, (tq, tk), 1)
    outs = []
    for hh in range(2):
        head_lanes = (lane >= hh * HEAD_DIM) & (lane < (hh + 1) * HEAD_DIM)
        qh = jnp.where(head_lanes, q, 0.0).astype(BF16)
        h = 2 * p + hh
        cq = jnp.sum(jnp.where(lane == h, cq_all, 0.0), axis=1, keepdims=True)
        m_ref[...] = jnp.full(m_ref.shape, NEG_INF, F32)
        l_ref[...] = jnp.zeros_like(l_ref)
        acc_ref[...] = jnp.zeros_like(acc_ref)

        def body(j, carry):
            k0 = pl.multiple_of(j * tk, tk)
            s = _dot_nt(qh, kb_ref[pl.ds(k0, tk), :])
            ck = ck_ref[0, pl.ds(h, 1), pl.ds(k0, tk)]
            s = s + (cq - ck)
            s = jnp.where(k0 + kcol <= qrow, s, NEG_INF)
            m_old = m_ref[...]
            m_new = jnp.maximum(m_old, jnp.max(s, axis=1, keepdims=True))
            alpha = jnp.exp(m_old - m_new)
            pr = jnp.exp(s - m_new)
            l_ref[...] = alpha * l_ref[...] + jnp.sum(pr, axis=1, keepdims=True)
            acc_ref[...] = alpha * acc_ref[...] + jnp.dot(
                pr.astype(BF16), vb_ref[pl.ds(k0, tk), :], preferred_element_type=F32)
            m_ref[...] = m_new
            return carry

        n_kv = (i * tq + tq + tk - 1) // tk
        lax.fori_loop(0, n_kv, body, 0)
        outs.append(acc_ref[...] / l_ref[...])
    o_ref[...] = jnp.where(lane < HEAD_DIM, outs[0], outs[1])


def _fox_prompt(proj, c_rows, c_t, n_seq, seq_len, tq, tk, col_q, col_k, col_v):
    nq = seq_len // tq
    n_pairs = N_HEADS // 2
    kern = functools.partial(_fox_prompt_kernel, scale=HEAD_DIM ** -0.5, tk=tk)
    return pl.pallas_call(
        kern,
        grid=(n_seq, n_pairs, nq),
        in_specs=[pl.BlockSpec((tq, LANES), lambda b, p, i: (b * nq + i, col_q + p)),
                  pl.BlockSpec((seq_len, LANES), lambda b, p, i: (b, col_k + p)),
                  pl.BlockSpec((seq_len, LANES), lambda b, p, i: (b, col_v + p)),
                  pl.BlockSpec((tq, LANES), lambda b, p, i: (b * nq + i, 0)),
                  pl.BlockSpec((1, N_HEADS, seq_len), lambda b, p, i: (b, 0, 0))],
        out_specs=pl.BlockSpec((tq, LANES), lambda b, p, i: (b * nq + i, p)),
        out_shape=jax.ShapeDtypeStruct((n_seq * seq_len, n_pairs * LANES), F32),
        scratch_shapes=[pltpu.VMEM((seq_len, LANES), BF16), pltpu.VMEM((seq_len, LANES), BF16),
                        pltpu.VMEM((tq, 1), F32), pltpu.VMEM((tq, 1), F32),
                        pltpu.VMEM((tq, LANES), F32)],
        compiler_params=_params(("parallel", "parallel", "arbitrary")),
        name="fox_prompt",
    )(proj, proj, proj, c_rows, c_t)


def _lane_prefix(x):
    n = x.shape[1]
    lane = lax.broadcasted_iota(jnp.int32, x.shape, 1)
    s = 1
    while s < n:
        x = x + jnp.where(lane >= s, pltpu.roll(x, s, axis=1), 0.0)
        s *= 2
    return x


def _page_prefix_kernel(lf_ref, p_ref, tot_ref):
    lf = lf_ref[...]
    p_ref[...] = _lane_prefix(lf)
    tot_ref[...] = jnp.broadcast_to(jnp.sum(lf, axis=1, keepdims=True), lf.shape)


def _page_prefix(lf_rows, tr):
    rows, page = lf_rows.shape
    spec = pl.BlockSpec((tr, page), lambda i: (i, 0))
    shp = jax.ShapeDtypeStruct((rows, page), F32)
    return pl.pallas_call(
        _page_prefix_kernel,
        grid=(rows // tr,),
        in_specs=[spec], out_specs=[spec, spec], out_shape=[shp, shp],
        compiler_params=_params(("parallel",)),
        name="page_prefix",
    )(lf_rows)


def _fox_sample_kernel(pt_ref, q_ref, *refs, scale, n_new, n_grp):
    k_refs = refs[0:n_grp]
    v_refs = refs[n_grp:2 * n_grp]
    p_refs = refs[2 * n_grp:3 * n_grp]
    t_refs = refs[3 * n_grp:4 * n_grp]
    kn_ref, vn_ref, lfn_ref, o_ref, qb_ref, m_ref, l_ref, acc_ref, cb_ref = refs[4 * n_grp:]
    j = pl.program_id(1)
    n_steps = pl.num_programs(1)
    n_tok, c = q_ref.shape[1], q_ref.shape[2]
    rows = n_tok * N_HEADS
    row_i = lax.broadcasted_iota(jnp.int32, (N_HEADS, c), 0)
    lane_i = lax.broadcasted_iota(jnp.int32, (N_HEADS, c), 1)
    own_lanes = (lane_i >= row_i * HEAD_DIM) & (lane_i < (row_i + 1) * HEAD_DIM)

    @pl.when(j == 0)
    def _():
        q = q_ref[0] * scale
        for t in range(n_tok):
            qt = jnp.broadcast_to(q[t:t + 1, :], (N_HEADS, c))
            qb_ref[t * N_HEADS:(t + 1) * N_HEADS, :] = jnp.where(own_lanes, qt, 0.0).astype(BF16)
        m_ref[...] = jnp.full(m_ref.shape, NEG_INF, F32)
        l_ref[...] = jnp.zeros_like(l_ref)
        acc_ref[...] = jnp.zeros_like(acc_ref)
        cb_ref[...] = jnp.zeros_like(cb_ref)

    qb = qb_ref[...]

    def update(s_list, v_list):
        m_old = m_ref[...]
        m_new = m_old
        for s in s_list:
            m_new = jnp.maximum(m_new, jnp.max(s, axis=1, keepdims=True))
        alpha = jnp.exp(m_old - m_new)
        l_new = alpha * l_ref[...]
        acc = alpha * acc_ref[...]
        for s, vv in zip(s_list, v_list):
            pr = jnp.exp(s - m_new)
            l_new = l_new + jnp.sum(pr, axis=1, keepdims=True)
            acc = acc + _dot_nt(pr.astype(BF16), vv.astype(BF16))
        l_ref[...] = l_new
        acc_ref[...] = acc
        m_ref[...] = m_new

    base = cb_ref[...]
    s_list, v_list = [], []
    for g in range(n_grp):
        c_k = base + p_refs[g][0, 0]
        base = base + t_refs[g][0, 0]
        s = jnp.dot(qb, k_refs[g][0, 0].astype(BF16), preferred_element_type=F32)
        s_list.append(s - jnp.concatenate([c_k] * n_tok, axis=0))
        v_list.append(v_refs[g][0, 0])
    cb_ref[...] = base
    update(s_list, v_list)

    @pl.when(j == n_steps - 1)
    def _():
        nk_new = kn_ref.shape[2]
        c_new = cb_ref[:, 0:nk_new] + _lane_prefix(lfn_ref[0])
        s = jnp.dot(qb, kn_ref[0].astype(BF16), preferred_element_type=F32)
        s = s - jnp.concatenate([c_new] * n_tok, axis=0)
        r_n = lax.broadcasted_iota(jnp.int32, (rows, nk_new), 0)
        c_n = lax.broadcasted_iota(jnp.int32, (rows, nk_new), 1)
        ok = (c_n * N_HEADS <= (r_n | (N_HEADS - 1))) & (c_n < n_new)
        update([jnp.where(ok, s, NEG_INF)], [vn_ref[0]])
        out = acc_ref[...] / l_ref[...]
        for t in range(n_tok):
            blk = out[t * N_HEADS:(t + 1) * N_HEADS, :]
            o_ref[0, t:t + 1, :] = jnp.sum(jnp.where(own_lanes, blk, 0.0), axis=0, keepdims=True)


def _fox_sample(page_table, layer, q, k_t, v_t, pfx, tot, k_new, v_new, lf_new, n_new, n_grp):
    db, dt, c = q.shape
    n_pages = page_table.shape[1]
    page = k_t.shape[3]
    nkn = k_new.shape[2]
    rows = dt * N_HEADS
    kern = functools.partial(_fox_sample_kernel, scale=HEAD_DIM ** -0.5, n_new=n_new, n_grp=n_grp)

    def paged(shape2, g):
        return pl.BlockSpec((1, 1) + shape2,
                            lambda b, j, pt: (layer, pt[b, j * n_grp + g], 0, 0))

    per_b = lambda shape2: pl.BlockSpec((1,) + shape2, lambda b, j, pt: (b, 0, 0))
    grid_spec = pltpu.PrefetchScalarGridSpec(
        num_scalar_prefetch=1,
        grid=(db, n_pages // n_grp),
        in_specs=[per_b((dt, c))]
        + [paged((c, page), g) for g in range(n_grp)]
        + [paged((c, page), g) for g in range(n_grp)]
        + [paged((N_HEADS, page), g) for g in range(n_grp)]
        + [paged((N_HEADS, page), g) for g in range(n_grp)]
        + [per_b((c, nkn)), per_b((c, nkn)), per_b((N_HEADS, nkn))],
        out_specs=per_b((dt, c)),
        scratch_shapes=[pltpu.VMEM((rows, c), BF16), pltpu.VMEM((rows, 1), F32),
                        pltpu.VMEM((rows, 1), F32), pltpu.VMEM((rows, c), F32),
                        pltpu.VMEM((N_HEADS, page), F32)],
    )
    return pl.pallas_call(
        kern,
        grid_spec=grid_spec,
        out_shape=jax.ShapeDtypeStruct((db, dt, c), F32),
        compiler_params=_params(("parallel", "arbitrary")),
        name="fox_sample",
    )(page_table, q, *([k_t] * n_grp), *([v_t] * n_grp), *([pfx] * n_grp), *([tot] * n_grp),
      k_new, v_new, lf_new)


def _sample_mix_kernel(xa_ref, sconv_ref, h0_ref, xc_ref, spool_ref, du_ref, dv_ref,
                       cw_ref, cb_ref, wa_ref, ba_ref, wx_ref, bx_ref, lam_ref,
                       pw_ref, psc_ref, lng_ref, lnb_ref, sw_ref, sb_ref,
                       oa_ref, oc_ref, vn_ref, od_ref, *, pos0):
    nt = xa_ref.shape[0]
    cw = cw_ref[...]
    ext = [sconv_ref[j] for j in range(CONV_W - 1)] + [xa_ref[t] for t in range(nt)]
    h = h0_ref[...]
    for t in range(nt):
        xc = cb_ref[...]
        for j in range(CONV_W):
            xc = xc + cw[j:j + 1, :] * ext[t + j]
        a, u = _lru_gates(xc, wa_ref[...], ba_ref[...], wx_ref[...], bx_ref[...], lam_ref[...])
        h = a * h + u
        oa_ref[t] = h
    pext = [spool_ref[j] for j in range(POOL_HIST)] + [xc_ref[t] for t in range(nt)]
    for t in range(nt):
        x_t = pext[POOL_HIST + t]
        for g, win in enumerate(POOL_WINDOWS):
            lo, hi = g * GROUP_W, (g + 1) * GROUP_W
            wsum = x_t[:, lo:hi]
            for j in range(1, win):
                wsum = wsum + pext[POOL_HIST + t - j][:, lo:hi]
            cnt = float(min(win, pos0 + t + 1))
            pooled = wsum / cnt - x_t[:, lo:hi]
            y = jnp.dot(pooled.astype(BF16), pw_ref[g], preferred_element_type=F32)
            oc_ref[t, :, lo:hi] = y * psc_ref[:, lo:hi]
    vns = []
    for t in range(nt):
        vn = _layer_norm(_gelu(dv_ref[t]), lng_ref[...], lnb_ref[...])
        vn_ref[t] = vn
        vns.append(vn)
    for t in range(nt):
        mixed = sb_ref[t]
        for s in range(t + 1):
            mixed = mixed + sw_ref[t, s] * vns[s]
        od_ref[t] = _gelu(du_ref[t]) * mixed


def _sample_mix(xa, sconv, h0, xc, spool, du, dv, cw, cb, wa, ba, wx, bx, lam,
                pw, psc, lng, lnb, sw, sb, pos0):
    shp = jax.ShapeDtypeStruct(xa.shape, F32)
    kern = functools.partial(_sample_mix_kernel, pos0=pos0)
    return pl.pallas_call(
        kern,
        out_shape=[shp, shp, shp, shp],
        compiler_params=pltpu.CompilerParams(vmem_limit_bytes=VMEM_LIMIT_V7X),
        name="sample_mix",
    )(xa, sconv, h0, xc, spool, du, dv, cw, cb, wa, ba, wx, bx, lam, pw, psc, lng, lnb, sw, sb)


def _merge_kernel(x_ref, oa_ref, ob_ref, oc_ref, od_ref, g0_ref, g1_ref, g2_ref, g3_ref,
                  wb_ref, o_ref):
    x = x_ref[...]
    acc = None
    for n, (o_n, g_n) in enumerate(((oa_ref, g0_ref), (ob_ref, g1_ref),
                                    (oc_ref, g2_ref), (od_ref, g3_ref))):
        gate = jax.nn.sigmoid(jnp.dot(x, g_n[0], preferred_element_type=F32))
        br = jnp.dot(o_n[...].astype(BF16), wb_ref[0, n], preferred_element_type=F32)
        acc = gate * br if acc is None else acc + gate * br
    o_ref[...] = acc.astype(o_ref.dtype)


def _merge(xb, oa, ob, oc, od, wg_all, gate_col0, wb_all, layer, tm, tn):
    m, d = xb.shape
    c = oa.shape[1]
    ncol = d // tn
    blk0 = gate_col0 // tn
    o_spec = pl.BlockSpec((tm, c), lambda i, j: (i, 0))
    g_specs = [pl.BlockSpec((1, d, tn), (lambda n: (lambda i, j: (layer, 0, blk0 + n * ncol + j)))(n))
               for n in range(N_BRANCH)]
    return pl.pallas_call(
        _merge_kernel,
        grid=(m // tm, ncol),
        in_specs=[pl.BlockSpec((tm, d), lambda i, j: (i, 0)), o_spec, o_spec, o_spec, o_spec]
        + g_specs + [pl.BlockSpec((1, N_BRANCH, c, tn), lambda i, j: (layer, 0, 0, j))],
        out_specs=pl.BlockSpec((tm, tn), lambda i, j: (i, j)),
        out_shape=jax.ShapeDtypeStruct((m, d), BF16),
        compiler_params=_params(("parallel", "arbitrary")),
        name="gated_merge",
    )(xb, oa, ob, oc, od, wg_all, wg_all, wg_all, wg_all, wb_all)


def _out_ln_kernel(m_ref, w_ref, x_ref, g_ref, b_ref, o_ref, ob_ref, *, alpha):
    y = jnp.dot(m_ref[...], w_ref[0], preferred_element_type=F32)
    out = _layer_norm(alpha * x_ref[...] + y, g_ref[...], b_ref[...])
    o_ref[...] = out
    ob_ref[...] = out.astype(BF16)


def _out_ln(merged, w_all, layer, x, g, b, alpha, tm):
    m, d = x.shape
    row = pl.BlockSpec((tm, d), lambda i: (i, 0))
    vec = pl.BlockSpec((1, d), lambda i: (0, 0))
    return pl.pallas_call(
        functools.partial(_out_ln_kernel, alpha=alpha),
        grid=(m // tm,),
        in_specs=[row, pl.BlockSpec((1, d, d), lambda i: (layer, 0, 0)), row, vec, vec],
        out_specs=[row, row],
        out_shape=[jax.ShapeDtypeStruct((m, d), F32), jax.ShapeDtypeStruct((m, d), BF16)],
        compiler_params=_params(("parallel",)),
        name="out_proj_ln",
    )(merged, w_all, x, g, b)


def _add_ln_kernel(y_ref, x_ref, g_ref, b_ref, o_ref, ob_ref, *, alpha):
    out = _layer_norm(alpha * x_ref[...] + y_ref[...], g_ref[...], b_ref[...])
    o_ref[...] = out
    ob_ref[...] = out.astype(BF16)


def _add_ln(y, x, g, b, alpha, tm):
    m, d = x.shape
    row = pl.BlockSpec((tm, d), lambda i: (i, 0))
    vec = pl.BlockSpec((1, d), lambda i: (0, 0))
    return pl.pallas_call(
        functools.partial(_add_ln_kernel, alpha=alpha),
        grid=(m // tm,),
        in_specs=[row, row, vec, vec],
        out_specs=[row, row],
        out_shape=[jax.ShapeDtypeStruct((m, d), F32), jax.ShapeDtypeStruct((m, d), BF16)],
        compiler_params=_params(("parallel",)),
        name="peer_add_ln",
    )(y, x, g, b)


LOG2E = 1.4426950408889634


def _top_rows(s, k):
    vals = []
    for _ in range(k):
        mx = jnp.max(s, axis=0, keepdims=True)
        vals.append(mx)
        s = jnp.where(s == mx, NEG_INF, s)
    return vals


def _peer_score_kernel(x_ref, wq_ref, k1_ref, k2_ref, s1_ref, s2_ref, st_ref):
    half = k1_ref.shape[1]
    tm = x_ref.shape[0]
    q = jnp.dot(x_ref[...], wq_ref[0], preferred_element_type=F32)
    s1 = _dot_nt(k1_ref[...], q[:, :half].astype(BF16)) * LOG2E
    s2 = _dot_nt(k2_ref[...], q[:, half:].astype(BF16)) * LOG2E
    for c0 in range(0, tm, LANES):
        a = s1[:, c0:c0 + LANES]
        b = s2[:, c0:c0 + LANES]
        v1 = _top_rows(a, PEER_TOPK + 1)
        v2 = _top_rows(b, PEER_TOPK + 1)
        m1, m2 = v1[0], v2[0]
        v2cat = jnp.concatenate([v - m2 for v in v2[:PEER_TOPK]], axis=0)
        cand = jnp.concatenate([(v - m1) + v2cat for v in v1[:PEER_TOPK]], axis=0)
        top = _top_rows(cand, PEER_TOPK + 1)
        z = jnp.exp2(top[0])
        for kk in range(1, PEER_TOPK):
            z = z + jnp.exp2(top[kk])
        lz = jnp.log(z) * LOG2E
        next_sum = jnp.maximum(top[PEER_TOPK],
                               jnp.maximum(v1[PEER_TOPK] - m1, v2[PEER_TOPK] - m2))
        tau = 0.5 * (top[PEER_TOPK - 1] + next_sum) - lz
        s1_ref[0, :, c0:c0 + LANES] = (a - m1) - lz
        s2_ref[0, :, c0:c0 + LANES] = b - m2
        st_ref[0, :, c0:c0 + LANES] = jnp.broadcast_to(tau, (8, LANES))


def _peer_scores(xb, wq_all, layer, k1, k2, tm):
    n, d = xb.shape
    nk, half = k1.shape
    big = pl.BlockSpec((1, nk, tm), lambda i, h: (h, 0, i))
    return pl.pallas_call(
        _peer_score_kernel,
        grid=(n // tm, N_HEADS),
        in_specs=[pl.BlockSpec((tm, d), lambda i, h: (i, 0)),
                  pl.BlockSpec((1, d, 2 * half), lambda i, h: (layer, 0, h)),
                  pl.BlockSpec(k1.shape, lambda i, h: (0, 0)),
                  pl.BlockSpec(k2.shape, lambda i, h: (0, 0))],
        out_specs=[big, big, pl.BlockSpec((1, 8, tm), lambda i, h: (h, 0, i))],
        out_shape=[jax.ShapeDtypeStruct((N_HEADS, nk, n), F32),
                   jax.ShapeDtypeStruct((N_HEADS, nk, n), F32),
                   jax.ShapeDtypeStruct((N_HEADS, 8, n), F32)],
        compiler_params=_params(("parallel", "arbitrary")),
        name="peer_scores",
    )(xb, wq_all, k1, k2)


PEER_LAG = 2


def _peer_dense_kernel(x_ref, u_ref, vt_ref, s1_ref, s2_ref, st_ref, y_ref,
                       acc_ref, h_new, h_old, g_new, g_old):
    j = pl.program_id(1)
    n_steps = pl.num_programs(1)
    n_e = n_steps - PEER_LAG
    te = u_ref.shape[1]
    nk = s2_ref.shape[1]

    @pl.when(j == 0)
    def _():
        acc_ref[...] = jnp.zeros_like(acc_ref)
        h_old[...] = jnp.zeros_like(h_old)
        g_old[...] = jnp.zeros_like(g_old)

    acc_ref[...] += jnp.dot(vt_ref[0], g_old[...], preferred_element_type=F32)

    tile = jnp.clip(j - 1, 0, n_e - 1)
    tm = x_ref.shape[0]
    for r in range(te // nk):
        i1 = tile * (te // nk) + r
        rs = slice(r * nk, (r + 1) * nk)
        s1_rows = [s1_ref[h, pl.ds(i1, 1), :] for h in range(N_HEADS)]
        for c0 in range(0, tm, LANES):
            ls = slice(c0, c0 + LANES)
            w = None
            for h in range(N_HEADS):
                arg = s1_rows[h][:, ls] + s2_ref[h, :, ls]
                term = jnp.where(arg >= st_ref[h, 0:1, ls], jnp.exp2(arg), 0.0)
                w = term if w is None else w + term
            g_new[rs, ls] = (_gelu(h_old[rs, ls]) * w).astype(BF16)

    h_new[...] = _dot_nt(u_ref[0], x_ref[...])

    g_old[...] = g_new[...]
    h_old[...] = h_new[...]

    @pl.when(j == n_steps - 1)
    def _():
        y_ref[...] = acc_ref[...].T


def _peer_dense(xb, u_all, vt_all, layer, s1t, s2t, st, tm, te):
    n_rows, d = xb.shape
    n_e = u_all.shape[1] // te
    nk = s1t.shape[1]
    tok3 = lambda i, j: (0, 0, i)
    return pl.pallas_call(
        _peer_dense_kernel,
        grid=(n_rows // tm, n_e + PEER_LAG),
        in_specs=[pl.BlockSpec((tm, d), lambda i, j: (i, 0)),
                  pl.BlockSpec((1, te, d), lambda i, j: (layer, jnp.minimum(j, n_e - 1), 0)),
                  pl.BlockSpec((1, d, te),
                               lambda i, j: (layer, 0, jnp.clip(j - PEER_LAG, 0, n_e - 1))),
                  pl.BlockSpec((N_HEADS, nk, tm), tok3),
                  pl.BlockSpec((N_HEADS, nk, tm), tok3),
                  pl.BlockSpec((N_HEADS, 8, tm), tok3)],
        out_specs=pl.BlockSpec((tm, d), lambda i, j: (i, 0)),
        out_shape=jax.ShapeDtypeStruct((n_rows, d), F32),
        scratch_shapes=[pltpu.VMEM((d, tm), F32),
                        pltpu.VMEM((te, tm), F32), pltpu.VMEM((te, tm), F32),
                        pltpu.VMEM((te, tm), BF16), pltpu.VMEM((te, tm), BF16)],
        compiler_params=_params(("parallel", "arbitrary")),
        name="peer_dense",
    )(xb, u_all, vt_all, s1t, s2t, st)


def _largest_divisor(n, cap, mult):
    best = None
    for t in range(mult, cap + 1, mult):
        if n % t == 0:
            best = t
    assert best is not None, (n, cap, mult)
    return best


def _block_diag(w):
    h, a, b = w.shape
    eye = jnp.eye(h, dtype=w.dtype)
    return (eye[:, None, :, None] * w[:, :, None, :]).reshape(h * a, h * b)


def _channel_mix(l, depth, x, xb, mixed, w, tm_merge, tm, te):
    alpha = (2 * depth) ** 0.25
    n = x.shape[0]
    merged = _merge(xb, *mixed, w["w_tail"], w["gate_col0"], w["wb"], l, tm_merge, 256)
    x1, x1b = _out_ln(merged, w["wo"], l, x, w["ln1_g"][l], w["ln1_b"][l], alpha,
                      _largest_divisor(n, 256, 16))
    s1t, s2t, st = _peer_scores(x1b, w["wq"], l, w["k1"][l], w["k2"][l], tm)
    y = _peer_dense(x1b, w["u"], w["vt"], l, s1t, s2t, st, tm, te)
    return _add_ln(y, x1, w["ln2_g"][l], w["ln2_b"][l], alpha, tm)


def kernel(x_prompt, x_sample, cache_k, cache_v, cache_logf, state_conv, state_lru, state_pool, page_table, w_in, conv_w, conv_b, lru_wa, lru_ba, lru_wx, lru_bx, lru_lambda, fox_bf, pool_w, pool_scale, sgu_ln_g, sgu_ln_b, sgu_w, sgu_b, w_branch, w_out, ln1_g, ln1_b, peer_wq, peer_k1, peer_k2, peer_u, peer_v, ln2_g, ln2_b):
    n_seq, seq_len, d = x_prompt.shape
    db, dt, _ = x_sample.shape
    depth = w_in.shape[0]
    c = d // N_BRANCH
    n_p, n_s = n_seq * seq_len, db * dt
    n_pool, page = cache_k.shape[1], cache_k.shape[2]
    n_pages = page_table.shape[1]
    past_len = n_pages * page
    assert c == N_HEADS * HEAD_DIM == N_BRANCH * GROUP_W
    assert seq_len % SGU_CHUNK == 0 and dt <= SGU_CHUNK and dt <= page and page == LANES

    fcol = 4 * c
    rows = lambda v: v.reshape(depth, 1, -1)
    w = dict(
        w_tail=w_in[:, :, fcol + N_HEADS:].astype(BF16),
        gate_col0=3 * c,
        wb=w_branch.astype(BF16), wo=w_out.astype(BF16), wq=peer_wq.astype(BF16),
        k1=peer_k1.astype(BF16), k2=peer_k2.astype(BF16),
        u=peer_u.astype(BF16), vt=peer_v.transpose(0, 2, 1).astype(BF16),
        ln1_g=rows(ln1_g), ln1_b=rows(ln1_b), ln2_g=rows(ln2_g), ln2_b=rows(ln2_b))
    w_head = w_in[:, :, :fcol].astype(BF16)
    w_f = jnp.pad(w_in[:, :, fcol:fcol + N_HEADS], ((0, 0), (0, 0), (0, LANES - N_HEADS))).astype(BF16)
    b_f = jnp.pad(fox_bf, ((0, 0), (0, LANES - N_HEADS))).reshape(depth, 1, LANES)
    wa = jax.vmap(_block_diag)(lru_wa).astype(BF16)
    wx = jax.vmap(_block_diag)(lru_wx).astype(BF16)
    cb, ba, bx, lam = rows(conv_b), rows(lru_ba), rows(lru_bx), rows(lru_lambda)
    pw = pool_w.astype(BF16)
    psc, lng, lnb = rows(pool_scale), rows(sgu_ln_g), rows(sgu_ln_b)

    k_t = cache_k.transpose(0, 1, 3, 4, 2).reshape(depth, n_pool, c, page)
    v_t = cache_v.transpose(0, 1, 3, 4, 2).reshape(depth, n_pool, c, page)
    lf_rows = cache_logf.transpose(0, 1, 3, 2).reshape(depth * n_pool * N_HEADS, page)
    pfx, tot = _page_prefix(lf_rows, _largest_divisor(lf_rows.shape[0], 2048, 8))
    pfx = pfx.reshape(depth, n_pool, N_HEADS, page)
    tot = tot.reshape(depth, n_pool, N_HEADS, page)

    tm_p = _largest_divisor(n_p, 1024, 16)
    tt = _largest_divisor(seq_len, 512, 8)
    te = 512
    n_grp = _largest_divisor(n_pages, 8, 1)
    hd = (N_HEADS, HEAD_DIM)
    tmaj = lambda a: a.reshape(db, dt, -1).transpose(1, 0, 2)
    bmaj = lambda a: a.transpose(1, 0, 2).reshape(n_s, -1)
    pad_lanes = lambda a: jnp.pad(a, ((0, 0), (0, 0), (0, LANES - a.shape[2])))

    xp, xs = x_prompt.reshape(n_p, d), x_sample.reshape(n_s, d)
    xpb, xsb = xp.astype(BF16), xs.astype(BF16)
    p_states, s_states = [], []
    for l in range(depth):
        ph = _matmul(xpb, w_head, l, fcol, tm_p, 512)
        pt = _matmul(xpb, w["w_tail"], l, 3 * c, tm_p, 512)
        sh = _matmul(xsb, w_head, l, fcol, n_s, 512)
        st = _matmul(xsb, w["w_tail"], l, 3 * c, n_s, 512)
        lf_p, c_p = _logf(xpb, w_f[l], b_f[l], n_seq, seq_len, tt)
        lf_s, _ = _logf(xsb, w_f[l], b_f[l], 1, n_s, n_s)
        c_t = c_p[:, :N_HEADS].reshape(n_seq, seq_len, N_HEADS).transpose(0, 2, 1)

        oa_p = _lru_prompt(ph, 0, n_seq, seq_len, tt, conv_w[l], cb[l], wa[l], ba[l], wx[l], bx[l],
                           lam[l])
        oc_p = _pool_prompt(pt, 0, n_seq, seq_len, tt, pw[l], psc[l])
        od_p = _sgu_prompt(pt, 1, 2, n_p, tt, lng[l], lnb[l], sgu_w[l], sgu_b[l].T)
        ob_p = _fox_prompt(ph, c_p, c_t, n_seq, seq_len, tt, tt,
                           c // LANES, 2 * c // LANES, 3 * c // LANES)

        sw = jnp.repeat(sgu_w[l, :, :dt, :dt].transpose(1, 2, 0), GROUP_W, axis=2).reshape(dt, dt, 1, c)
        sb = jnp.repeat(sgu_b[l, :, :dt].T, GROUP_W, axis=1).reshape(dt, 1, c)
        oa_s, oc_s, vn_s, od_s = _sample_mix(
            tmaj(sh[:, :c]), state_conv[l].transpose(1, 0, 2), state_lru[l],
            tmaj(st[:, :c]), state_pool[l].transpose(1, 0, 2),
            tmaj(st[:, c:2 * c]), tmaj(st[:, 2 * c:]),
            conv_w[l], cb[l], wa[l], ba[l], wx[l], bx[l], lam[l], pw[l], psc[l], lng[l], lnb[l],
            sw, sb, past_len)
        new_t = lambda a: pad_lanes(a.reshape(db, dt, -1).transpose(0, 2, 1))
        ob_s = _fox_sample(page_table, l, sh[:, c:2 * c].reshape(db, dt, c), k_t, v_t, pfx, tot,
                           new_t(sh[:, 2 * c:3 * c]), new_t(sh[:, 3 * c:]),
                           new_t(lf_s[:, :N_HEADS]), dt, n_grp).reshape(n_s, c)

        p3 = lambda a: a.reshape(n_seq, seq_len, -1)
        p_states.append((
            ph[:, 2 * c:3 * c].reshape(n_seq, seq_len, *hd),
            ph[:, 3 * c:].reshape(n_seq, seq_len, *hd),
            lf_p[:, :N_HEADS].reshape(n_seq, seq_len, N_HEADS),
            p3(ph)[:, seq_len - (CONV_W - 1):, :c],
            p3(oa_p)[:, -1],
            p3(pt)[:, seq_len - POOL_HIST:, :c]))
        s3 = lambda a: a.reshape(db, dt, -1)
        s_states.append((
            sh[:, 2 * c:3 * c].reshape(db, dt, *hd),
            sh[:, 3 * c:].reshape(db, dt, *hd),
            lf_s[:, :N_HEADS].reshape(db, dt, N_HEADS),
            jnp.concatenate([state_conv[l], s3(sh)[:, :, :c]], axis=1)[:, -(CONV_W - 1):],
            oa_s[dt - 1],
            jnp.concatenate([state_pool[l], s3(st)[:, :, :c]], axis=1)[:, -POOL_HIST:],
            vn_s.transpose(1, 0, 2)))

        xp, xpb = _channel_mix(l, depth, xp, xpb, (oa_p, ob_p, oc_p, od_p), w, tm_p, 512, te)
        xs, xsb = _channel_mix(l, depth, xs, xsb,
                               (bmaj(oa_s), ob_s, bmaj(oc_s), bmaj(od_s)), w, n_s, n_s, te)

    yp = xp.reshape(n_seq, seq_len, d)
    ys = xs.reshape(db, dt, d)
    p_out = [jnp.stack([s[j] for s in p_states]) for j in range(6)]
    s_out = [jnp.stack([s[j] for s in s_states]) for j in range(7)]
    return (yp, ys, *p_out, *s_out)
```

```python
import functools

import jax
import jax.numpy as jnp
from jax import lax
from jax.experimental import pallas as pl
from jax.experimental.pallas import tpu as pltpu

F32 = jnp.float32
BF16 = jnp.bfloat16

N_BRANCH = 4
N_HEADS = 8
HEAD_DIM = 64
CONV_W = 4
LRU_C = 8.0
POOL_WINDOWS = (2, 4, 8, 16)
POOL_HIST = 15
GROUP_W = 128
SGU_CHUNK = 128
PEER_NKEYS = 128
PEER_TOPK = 16
LN_EPS = 1e-5
NEG_INF = float("-inf")

VMEM_LIMIT_V7X = 56 * 1024 * 1024
LANES = 128


def _params(semantics):
    return pltpu.CompilerParams(dimension_semantics=semantics, vmem_limit_bytes=VMEM_LIMIT_V7X)


def _gelu(x):
    return 0.5 * x * (1.0 + lax.erf(x * 0.7071067811865476))


def _softplus(z):
    return jnp.maximum(z, 0.0) + jnp.log1p(jnp.exp(-jnp.abs(z)))


def _log_sigmoid(z):
    return -_softplus(-z)


def _expm1(y):
    e = jnp.exp(y)
    return jnp.where(e == 1.0, y, (e - 1.0) * y / jnp.log(e))


def _layer_norm(x, g, b):
    mu = jnp.mean(x, axis=-1, keepdims=True)
    d = x - mu
    var = jnp.mean(d * d, axis=-1, keepdims=True)
    return d * lax.rsqrt(var + LN_EPS) * g + b


def _cumsum_rows(x):
    n = x.shape[0]
    rows = lax.broadcasted_iota(jnp.int32, x.shape, 0)
    s = 1
    while s < n:
        x = x + jnp.where(rows >= s, pltpu.roll(x, s, axis=0), 0.0)
        s *= 2
    return x


def _dot_nt(a, b):
    return lax.dot_general(a, b, (((1,), (1,)), ((), ())), preferred_element_type=F32)


def _matmul_kernel(x_ref, w_ref, o_ref):
    o_ref[...] = jnp.dot(x_ref[...], w_ref[0], preferred_element_type=F32)


def _matmul(xb, w_all, layer, n_cols, tm, tn):
    m, k = xb.shape
    return pl.pallas_call(
        _matmul_kernel,
        grid=(m // tm, n_cols // tn),
        in_specs=[pl.BlockSpec((tm, k), lambda i, j: (i, 0)),
                  pl.BlockSpec((1, k, tn), lambda i, j: (layer, 0, j))],
        out_specs=pl.BlockSpec((tm, tn), lambda i, j: (i, j)),
        out_shape=jax.ShapeDtypeStruct((m, n_cols), F32),
        compiler_params=_params(("parallel", "arbitrary")),
        name="in_proj",
    )(xb, w_all)


def _logf_kernel(x_ref, w_ref, b_ref, lf_ref, c_ref, carry_ref):
    i = pl.program_id(1)
    fl = jnp.dot(x_ref[...], w_ref[...], preferred_element_type=F32) + b_ref[...]
    lf = _log_sigmoid(fl)
    lf_ref[...] = lf

    @pl.when(i == 0)
    def _():
        carry_ref[...] = jnp.zeros_like(carry_ref)

    c = _cumsum_rows(lf) + carry_ref[...]
    c_ref[...] = c
    carry_ref[...] = c[c.shape[0] - 1:, :]


def _logf(xb, w_f, b_f, n_seq, seq_len, tt):
    k = xb.shape[1]
    nt = seq_len // tt
    rows = n_seq * seq_len
    return pl.pallas_call(
        _logf_kernel,
        grid=(n_seq, nt),
        in_specs=[pl.BlockSpec((tt, k), lambda b, i: (b * nt + i, 0)),
                  pl.BlockSpec((k, LANES), lambda b, i: (0, 0)),
                  pl.BlockSpec((1, LANES), lambda b, i: (0, 0))],
        out_specs=[pl.BlockSpec((tt, LANES), lambda b, i: (b * nt + i, 0)),
                   pl.BlockSpec((tt, LANES), lambda b, i: (b * nt + i, 0))],
        out_shape=[jax.ShapeDtypeStruct((rows, LANES), F32),
                   jax.ShapeDtypeStruct((rows, LANES), F32)],
        scratch_shapes=[pltpu.VMEM((1, LANES), F32)],
        compiler_params=_params(("parallel", "arbitrary")),
        name="log_forget",
    )(xb, w_f, b_f)


HALO = 8


def _lru_gates(xc, wa, ba, wx, bx, lam):
    xcb = xc.astype(BF16)
    r = jax.nn.sigmoid(jnp.dot(xcb, wa, preferred_element_type=F32) + ba)
    ig = jax.nn.sigmoid(jnp.dot(xcb, wx, preferred_element_type=F32) + bx)
    log_a = -LRU_C * r * _softplus(-lam)
    a = jnp.exp(log_a)
    u = jnp.sqrt(-_expm1(2.0 * log_a)) * (ig * xc)
    return a, u


def _lru_prompt_kernel(xa_ref, cw_ref, cb_ref, wa_ref, ba_ref, wx_ref, bx_ref, lam_ref,
                       o_ref, ext_ref, h_ref, a_ref, u_ref):
    i = pl.program_id(1)
    tt = xa_ref.shape[0]

    @pl.when(i == 0)
    def _():
        ext_ref[0:HALO, :] = jnp.zeros((HALO, ext_ref.shape[1]), F32)
        h_ref[...] = jnp.zeros_like(h_ref)

    @pl.when(i > 0)
    def _():
        ext_ref[0:HALO, :] = ext_ref[tt:tt + HALO, :]

    ext_ref[HALO:HALO + tt, :] = xa_ref[...]
    cw = cw_ref[...]
    xc = cb_ref[...]
    for j in range(CONV_W):
        off = HALO - (CONV_W - 1) + j
        xc = xc + cw[j:j + 1, :] * ext_ref[off:off + tt, :]
    a, u = _lru_gates(xc, wa_ref[...], ba_ref[...], wx_ref[...], bx_ref[...], lam_ref[...])
    a_ref[...] = a
    u_ref[...] = u

    def body(t, h):
        h = a_ref[pl.ds(t, 1), :] * h + u_ref[pl.ds(t, 1), :]
        o_ref[pl.ds(t, 1), :] = h
        return h

    h_ref[...] = lax.fori_loop(0, tt, body, h_ref[...])


def _lru_prompt(proj, col_blk, n_seq, seq_len, tt, cw, cb, wa, ba, wx, bx, lam):
    c = cw.shape[1]
    nt = seq_len // tt
    const = lambda b, i: (0, 0)
    return pl.pallas_call(
        _lru_prompt_kernel,
        grid=(n_seq, nt),
        in_specs=[pl.BlockSpec((tt, c), lambda b, i: (b * nt + i, col_blk)),
                  pl.BlockSpec((CONV_W, c), const), pl.BlockSpec((1, c), const),
                  pl.BlockSpec((c, c), const), pl.BlockSpec((1, c), const),
                  pl.BlockSpec((c, c), const), pl.BlockSpec((1, c), const),
                  pl.BlockSpec((1, c), const)],
        out_specs=pl.BlockSpec((tt, c), lambda b, i: (b * nt + i, 0)),
        out_shape=jax.ShapeDtypeStruct((n_seq * seq_len, c), F32),
        scratch_shapes=[pltpu.VMEM((HALO + tt, c), F32), pltpu.VMEM((1, c), F32),
                        pltpu.VMEM((tt, c), F32), pltpu.VMEM((tt, c), F32)],
        compiler_params=_params(("parallel", "arbitrary")),
        name="lru_prompt",
    )(proj, cw, cb, wa, ba, wx, bx, lam)


POOL_HALO = 16


def _pool_prompt_kernel(x_ref, w_ref, sc_ref, o_ref, ext_ref):
    i = pl.program_id(1)
    tt = x_ref.shape[0]

    @pl.when(i == 0)
    def _():
        ext_ref[0:POOL_HALO, :] = jnp.zeros((POOL_HALO, ext_ref.shape[1]), F32)

    @pl.when(i > 0)
    def _():
        ext_ref[0:POOL_HALO, :] = ext_ref[tt:tt + POOL_HALO, :]

    ext_ref[POOL_HALO:POOL_HALO + tt, :] = x_ref[...]
    pos = i * tt + lax.broadcasted_iota(jnp.int32, (tt, GROUP_W), 0)
    for g, win in enumerate(POOL_WINDOWS):
        lo, hi = g * GROUP_W, (g + 1) * GROUP_W
        wsum = ext_ref[POOL_HALO:POOL_HALO + tt, lo:hi]
        for j in range(1, win):
            wsum = wsum + ext_ref[POOL_HALO - j:POOL_HALO - j + tt, lo:hi]
        cnt = jnp.minimum(win, pos + 1).astype(F32)
        pooled = wsum / cnt - x_ref[:, lo:hi]
        y = jnp.dot(pooled.astype(BF16), w_ref[g], preferred_element_type=F32)
        o_ref[:, lo:hi] = y * sc_ref[:, lo:hi]


def _pool_prompt(proj, col_blk, n_seq, seq_len, tt, w, scale):
    c = scale.shape[1]
    nt = seq_len // tt
    return pl.pallas_call(
        _pool_prompt_kernel,
        grid=(n_seq, nt),
        in_specs=[pl.BlockSpec((tt, c), lambda b, i: (b * nt + i, col_blk)),
                  pl.BlockSpec(w.shape, lambda b, i: (0, 0, 0)),
                  pl.BlockSpec((1, c), lambda b, i: (0, 0))],
        out_specs=pl.BlockSpec((tt, c), lambda b, i: (b * nt + i, 0)),
        out_shape=jax.ShapeDtypeStruct((n_seq * seq_len, c), F32),
        scratch_shapes=[pltpu.VMEM((POOL_HALO + tt, c), F32)],
        compiler_params=_params(("parallel", "arbitrary")),
        name="pool_prompt",
    )(proj, w, scale)


def _sgu_prompt_kernel(du_ref, dv_ref, g_ref, b_ref, w_ref, sbt_ref, o_ref):
    tt = du_ref.shape[0]
    vn = _layer_norm(_gelu(dv_ref[...]), g_ref[...], b_ref[...])
    vnb = vn.astype(BF16)
    gu = _gelu(du_ref[...])
    row = lax.broadcasted_iota(jnp.int32, (SGU_CHUNK, SGU_CHUNK), 0)
    col = lax.broadcasted_iota(jnp.int32, (SGU_CHUNK, SGU_CHUNK), 1)
    sbt = sbt_ref[...]
    for g in range(N_BRANCH):
        lo, hi = g * GROUP_W, (g + 1) * GROUP_W
        wt = jnp.where(col <= row, w_ref[g], 0.0).astype(BF16)
        bcol = sbt[:, g:g + 1]
        for ch in range(tt // SGU_CHUNK):
            r0, r1 = ch * SGU_CHUNK, (ch + 1) * SGU_CHUNK
            mixed = jnp.dot(wt, vnb[r0:r1, lo:hi], preferred_element_type=F32) + bcol
            o_ref[r0:r1, lo:hi] = gu[r0:r1, lo:hi] * mixed


def _sgu_prompt(proj, col_u, col_v, n_rows, tt, ln_g, ln_b, w, sbt):
    c = ln_g.shape[1]
    return pl.pallas_call(
        _sgu_prompt_kernel,
        grid=(n_rows // tt,),
        in_specs=[pl.BlockSpec((tt, c), lambda i: (i, col_u)),
                  pl.BlockSpec((tt, c), lambda i: (i, col_v)),
                  pl.BlockSpec((1, c), lambda i: (0, 0)),
                  pl.BlockSpec((1, c), lambda i: (0, 0)),
                  pl.BlockSpec(w.shape, lambda i: (0, 0, 0)),
                  pl.BlockSpec(sbt.shape, lambda i: (0, 0))],
        out_specs=pl.BlockSpec((tt, c), lambda i: (i, 0)),
        out_shape=jax.ShapeDtypeStruct((n_rows, c), F32),
        compiler_params=_params(("parallel",)),
        name="sgu_prompt",
    )(proj, proj, ln_g, ln_b, w, sbt)


def _fox_prompt_kernel(q_ref, k_ref, v_ref, cq_ref, ck_ref, o_ref, kb_ref, vb_ref,
                       m_ref, l_ref, acc_ref, *, scale, tk):
    p = pl.program_id(1)
    i = pl.program_id(2)
    tq = q_ref.shape[0]

    @pl.when(i == 0)
    def _():
        kb_ref[...] = k_ref[...].astype(BF16)
        vb_ref[...] = v_ref[...].astype(BF16)

    lane = lax.broadcasted_iota(jnp.int32, (tq, LANES), 1)
    q = q_ref[...] * scale
    cq_all = cq_ref[...]
    qh, cq = [], []
    for hh in range(2):
        head_lanes = (lane >= hh * HEAD_DIM) & (lane < (hh + 1) * HEAD_DIM)
        qh.append(jnp.where(head_lanes, q, 0.0).astype(BF16))
        cq.append(jnp.sum(jnp.where(lane == 2 * p + hh, cq_all, 0.0), axis=1, keepdims=True))
    m_ref[...] = jnp.full(m_ref.shape, NEG_INF, F32)
    l_ref[...] = jnp.zeros_like(l_ref)
    acc_ref[...] = jnp.zeros_like(acc_ref)

    def chunk(j, on_diagonal):
        k0 = pl.multiple_of(j * tk, tk)
        kk = kb_ref[pl.ds(k0, tk), :]
        vv = vb_ref[pl.ds(k0, tk), :]
        for hh in range(2):
            s = _dot_nt(qh[hh], kk)
            ck = ck_ref[0, pl.ds(2 * p + hh, 1), pl.ds(k0, tk)]
            s = s + (cq[hh] - ck)
            if on_diagonal:
                qrow = lax.broadcasted_iota(jnp.int32, (tq, tk), 0)
                kcol = lax.broadcasted_iota(jnp.int32, (tq, tk), 1)
                s = jnp.where(kcol <= qrow, s, NEG_INF)
            m_old = m_ref[hh]
            m_new = jnp.maximum(m_old, jnp.max(s, axis=1, keepdims=True))
            alpha = jnp.exp(m_old - m_new)
            pr = jnp.concatenate([jnp.exp(s[:, c0:c0 + LANES] - m_new)
                                  for c0 in range(0, tk, LANES)], axis=1)
            l_ref[hh] = alpha * l_ref[hh] + jnp.sum(pr, axis=1, keepdims=True)
            acc_ref[hh] = alpha * acc_ref[hh] + jnp.dot(pr.astype(BF16), vv,
                                                        preferred_element_type=F32)
            m_ref[hh] = m_new

    def body(j, carry):
        chunk(j, False)
        return carry

    lax.fori_loop(0, i, body, 0)
    chunk(i, True)
    o_ref[...] = jnp.where(lane < HEAD_DIM, acc_ref[0] / l_ref[0], acc_ref[1] / l_ref[1])


def _fox_prompt(proj, c_rows, c_t, n_seq, seq_len, tq, tk, col_q, col_k, col_v):
    assert tq == tk
    nq = seq_len // tq
    n_pairs = N_HEADS // 2
    kern = functools.partial(_fox_prompt_kernel, scale=HEAD_DIM ** -0.5, tk=tk)
    return pl.pallas_call(
        kern,
        grid=(n_seq, n_pairs, nq),
        in_specs=[pl.BlockSpec((tq, LANES), lambda b, p, i: (b * nq + i, col_q + p)),
                  pl.BlockSpec((seq_len, LANES), lambda b, p, i: (b, col_k + p)),
                  pl.BlockSpec((seq_len, LANES), lambda b, p, i: (b, col_v + p)),
                  pl.BlockSpec((tq, LANES), lambda b, p, i: (b * nq + i, 0)),
                  pl.BlockSpec((1, N_HEADS, seq_len), lambda b, p, i: (b, 0, 0))],
        out_specs=pl.BlockSpec((tq, LANES), lambda b, p, i: (b * nq + i, p)),
        out_shape=jax.ShapeDtypeStruct((n_seq * seq_len, n_pairs * LANES), F32),
        scratch_shapes=[pltpu.VMEM((seq_len, LANES), BF16), pltpu.VMEM((seq_len, LANES), BF16),
                        pltpu.VMEM((2, tq, LANES), F32), pltpu.VMEM((2, tq, LANES), F32),
                        pltpu.VMEM((2, tq, LANES), F32)],
        compiler_params=_params(("parallel", "parallel", "arbitrary")),
        name="fox_prompt",
    )(proj, proj, proj, c_rows, c_t)


def _lane_prefix(x):
    n = x.shape[1]
    lane = lax.broadcasted_iota(jnp.int32, x.shape, 1)
    s = 1
    while s < n:
        x = x + jnp.where(lane >= s, pltpu.roll(x, s, axis=1), 0.0)
        s *= 2
    return x


def _page_prefix_kernel(lf_ref, p_ref, tot_ref):
    lf = lf_ref[...]
    p_ref[...] = _lane_prefix(lf)
    tot_ref[...] = jnp.broadcast_to(jnp.sum(lf, axis=1, keepdims=True), lf.shape)


def _page_prefix(lf_rows, tr):
    rows, page = lf_rows.shape
    spec = pl.BlockSpec((tr, page), lambda i: (i, 0))
    shp = jax.ShapeDtypeStruct((rows, page), F32)
    return pl.pallas_call(
        _page_prefix_kernel,
        grid=(rows // tr,),
        in_specs=[spec], out_specs=[spec, spec], out_shape=[shp, shp],
        compiler_params=_params(("parallel",)),
        name="page_prefix",
    )(lf_rows)


def _fox_sample_kernel(pt_ref, q_ref, *refs, scale, n_new, n_grp):
    k_refs = refs[0:n_grp]
    v_refs = refs[n_grp:2 * n_grp]
    p_refs = refs[2 * n_grp:3 * n_grp]
    t_refs = refs[3 * n_grp:4 * n_grp]
    kn_ref, vn_ref, lfn_ref, o_ref, qb_ref, m_ref, l_ref, acc_ref, cb_ref = refs[4 * n_grp:]
    j = pl.program_id(1)
    n_steps = pl.num_programs(1)
    n_tok, c = q_ref.shape[1], q_ref.shape[2]
    rows = n_tok * N_HEADS
    row_i = lax.broadcasted_iota(jnp.int32, (N_HEADS, c), 0)
    lane_i = lax.broadcasted_iota(jnp.int32, (N_HEADS, c), 1)
    own_lanes = (lane_i >= row_i * HEAD_DIM) & (lane_i < (row_i + 1) * HEAD_DIM)

    @pl.when(j == 0)
    def _():
        q = q_ref[0] * scale
        for t in range(n_tok):
            qt = jnp.broadcast_to(q[t:t + 1, :], (N_HEADS, c))
            qb_ref[t * N_HEADS:(t + 1) * N_HEADS, :] = jnp.where(own_lanes, qt, 0.0).astype(BF16)
        m_ref[...] = jnp.full(m_ref.shape, NEG_INF, F32)
        l_ref[...] = jnp.zeros_like(l_ref)
        acc_ref[...] = jnp.zeros_like(acc_ref)
        cb_ref[...] = jnp.zeros_like(cb_ref)

    qb = qb_ref[...]

    def update(s_list, v_list):
        m_old = m_ref[...]
        m_new = m_old
        for s in s_list:
            m_new = jnp.maximum(m_new, jnp.max(s, axis=1, keepdims=True))
        alpha = jnp.exp(m_old - m_new)
        l_new = alpha * l_ref[...]
        acc = alpha * acc_ref[...]
        for s, vv in zip(s_list, v_list):
            pr = jnp.exp(s - m_new)
            l_new = l_new + jnp.sum(pr, axis=1, keepdims=True)
            acc = acc + _dot_nt(pr.astype(BF16), vv.astype(BF16))
        l_ref[...] = l_new
        acc_ref[...] = acc
        m_ref[...] = m_new

    base = cb_ref[...]
    s_list, v_list = [], []
    for g in range(n_grp):
        c_k = base + p_refs[g][0, 0]
        base = base + t_refs[g][0, 0]
        s = jnp.dot(qb, k_refs[g][0, 0].astype(BF16), preferred_element_type=F32)
        s_list.append(s - jnp.concatenate([c_k] * n_tok, axis=0))
        v_list.append(v_refs[g][0, 0])
    cb_ref[...] = base
    update(s_list, v_list)

    @pl.when(j == n_steps - 1)
    def _():
        nk_new = kn_ref.shape[2]
        c_new = cb_ref[:, 0:nk_new] + _lane_prefix(lfn_ref[0])
        s = jnp.dot(qb, kn_ref[0].astype(BF16), preferred_element_type=F32)
        s = s - jnp.concatenate([c_new] * n_tok, axis=0)
        r_n = lax.broadcasted_iota(jnp.int32, (rows, nk_new), 0)
        c_n = lax.broadcasted_iota(jnp.int32, (rows, nk_new), 1)
        ok = (c_n * N_HEADS <= (r_n | (N_HEADS - 1))) & (c_n < n_new)
        update([jnp.where(ok, s, NEG_INF)], [vn_ref[0]])
        out = acc_ref[...] / l_ref[...]
        for t in range(n_tok):
            blk = out[t * N_HEADS:(t + 1) * N_HEADS, :]
            o_ref[0, t:t + 1, :] = jnp.sum(jnp.where(own_lanes, blk, 0.0), axis=0, keepdims=True)


def _fox_sample(page_table, layer, q, k_t, v_t, pfx, tot, k_new, v_new, lf_new, n_new, n_grp):
    db, dt, c = q.shape
    n_pages = page_table.shape[1]
    page = k_t.shape[3]
    nkn = k_new.shape[2]
    rows = dt * N_HEADS
    kern = functools.partial(_fox_sample_kernel, scale=HEAD_DIM ** -0.5, n_new=n_new, n_grp=n_grp)

    def paged(shape2, g):
        return pl.BlockSpec((1, 1) + shape2,
                            lambda b, j, pt: (layer, pt[b, j * n_grp + g], 0, 0))

    per_b = lambda shape2: pl.BlockSpec((1,) + shape2, lambda b, j, pt: (b, 0, 0))
    grid_spec = pltpu.PrefetchScalarGridSpec(
        num_scalar_prefetch=1,
        grid=(db, n_pages // n_grp),
        in_specs=[per_b((dt, c))]
        + [paged((c, page), g) for g in range(n_grp)]
        + [paged((c, page), g) for g in range(n_grp)]
        + [paged((N_HEADS, page), g) for g in range(n_grp)]
        + [paged((N_HEADS, page), g) for g in range(n_grp)]
        + [per_b((c, nkn)), per_b((c, nkn)), per_b((N_HEADS, nkn))],
        out_specs=per_b((dt, c)),
        scratch_shapes=[pltpu.VMEM((rows, c), BF16), pltpu.VMEM((rows, 1), F32),
                        pltpu.VMEM((rows, 1), F32), pltpu.VMEM((rows, c), F32),
                        pltpu.VMEM((N_HEADS, page), F32)],
    )
    return pl.pallas_call(
        kern,
        grid_spec=grid_spec,
        out_shape=jax.ShapeDtypeStruct((db, dt, c), F32),
        compiler_params=_params(("parallel", "arbitrary")),
        name="fox_sample",
    )(page_table, q, *([k_t] * n_grp), *([v_t] * n_grp), *([pfx] * n_grp), *([tot] * n_grp),
      k_new, v_new, lf_new)


def _sample_mix_kernel(xa_ref, sconv_ref, h0_ref, xc_ref, spool_ref, du_ref, dv_ref,
                       cw_ref, cb_ref, wa_ref, ba_ref, wx_ref, bx_ref, lam_ref,
                       pw_ref, psc_ref, lng_ref, lnb_ref, sw_ref, sb_ref,
                       oa_ref, oc_ref, vn_ref, od_ref, *, pos0):
    nt = xa_ref.shape[0]
    cw = cw_ref[...]
    ext = [sconv_ref[j] for j in range(CONV_W - 1)] + [xa_ref[t] for t in range(nt)]
    h = h0_ref[...]
    for t in range(nt):
        xc = cb_ref[...]
        for j in range(CONV_W):
            xc = xc + cw[j:j + 1, :] * ext[t + j]
        a, u = _lru_gates(xc, wa_ref[...], ba_ref[...], wx_ref[...], bx_ref[...], lam_ref[...])
        h = a * h + u
        oa_ref[t] = h
    pext = [spool_ref[j] for j in range(POOL_HIST)] + [xc_ref[t] for t in range(nt)]
    for t in range(nt):
        x_t = pext[POOL_HIST + t]
        for g, win in enumerate(POOL_WINDOWS):
            lo, hi = g * GROUP_W, (g + 1) * GROUP_W
            wsum = x_t[:, lo:hi]
            for j in range(1, win):
                wsum = wsum + pext[POOL_HIST + t - j][:, lo:hi]
            cnt = float(min(win, pos0 + t + 1))
            pooled = wsum / cnt - x_t[:, lo:hi]
            y = jnp.dot(pooled.astype(BF16), pw_ref[g], preferred_element_type=F32)
            oc_ref[t, :, lo:hi] = y * psc_ref[:, lo:hi]
    vns = []
    for t in range(nt):
        vn = _layer_norm(_gelu(dv_ref[t]), lng_ref[...], lnb_ref[...])
        vn_ref[t] = vn
        vns.append(vn)
    for t in range(nt):
        mixed = sb_ref[t]
        for s in range(t + 1):
            mixed = mixed + sw_ref[t, s] * vns[s]
        od_ref[t] = _gelu(du_ref[t]) * mixed


def _sample_mix(xa, sconv, h0, xc, spool, du, dv, cw, cb, wa, ba, wx, bx, lam,
                pw, psc, lng, lnb, sw, sb, pos0):
    shp = jax.ShapeDtypeStruct(xa.shape, F32)
    kern = functools.partial(_sample_mix_kernel, pos0=pos0)
    return pl.pallas_call(
        kern,
        out_shape=[shp, shp, shp, shp],
        compiler_params=pltpu.CompilerParams(vmem_limit_bytes=VMEM_LIMIT_V7X),
        name="sample_mix",
    )(xa, sconv, h0, xc, spool, du, dv, cw, cb, wa, ba, wx, bx, lam, pw, psc, lng, lnb, sw, sb)


def _merge_kernel(x_ref, oa_ref, ob_ref, oc_ref, od_ref, g0_ref, g1_ref, g2_ref, g3_ref,
                  wb_ref, o_ref):
    x = x_ref[...]
    acc = None
    for n, (o_n, g_n) in enumerate(((oa_ref, g0_ref), (ob_ref, g1_ref),
                                    (oc_ref, g2_ref), (od_ref, g3_ref))):
        gate = jax.nn.sigmoid(jnp.dot(x, g_n[0], preferred_element_type=F32))
        br = jnp.dot(o_n[...].astype(BF16), wb_ref[0, n], preferred_element_type=F32)
        acc = gate * br if acc is None else acc + gate * br
    o_ref[...] = acc.astype(o_ref.dtype)


def _merge(xb, oa, ob, oc, od, wg_all, gate_col0, wb_all, layer, tm, tn):
    m, d = xb.shape
    c = oa.shape[1]
    ncol = d // tn
    blk0 = gate_col0 // tn
    o_spec = pl.BlockSpec((tm, c), lambda i, j: (i, 0))
    g_specs = [pl.BlockSpec((1, d, tn), (lambda n: (lambda i, j: (layer, 0, blk0 + n * ncol + j)))(n))
               for n in range(N_BRANCH)]
    return pl.pallas_call(
        _merge_kernel,
        grid=(m // tm, ncol),
        in_specs=[pl.BlockSpec((tm, d), lambda i, j: (i, 0)), o_spec, o_spec, o_spec, o_spec]
        + g_specs + [pl.BlockSpec((1, N_BRANCH, c, tn), lambda i, j: (layer, 0, 0, j))],
        out_specs=pl.BlockSpec((tm, tn), lambda i, j: (i, j)),
        out_shape=jax.ShapeDtypeStruct((m, d), BF16),
        compiler_params=_params(("parallel", "arbitrary")),
        name="gated_merge",
    )(xb, oa, ob, oc, od, wg_all, wg_all, wg_all, wg_all, wb_all)


def _out_ln_kernel(m_ref, w_ref, x_ref, g_ref, b_ref, o_ref, ob_ref, *, alpha):
    y = jnp.dot(m_ref[...], w_ref[0], preferred_element_type=F32)
    out = _layer_norm(alpha * x_ref[...] + y, g_ref[...], b_ref[...])
    o_ref[...] = out
    ob_ref[...] = out.astype(BF16)


def _out_ln(merged, w_all, layer, x, g, b, alpha, tm):
    m, d = x.shape
    row = pl.BlockSpec((tm, d), lambda i: (i, 0))
    vec = pl.BlockSpec((1, d), lambda i: (0, 0))
    return pl.pallas_call(
        functools.partial(_out_ln_kernel, alpha=alpha),
        grid=(m // tm,),
        in_specs=[row, pl.BlockSpec((1, d, d), lambda i: (layer, 0, 0)), row, vec, vec],
        out_specs=[row, row],
        out_shape=[jax.ShapeDtypeStruct((m, d), F32), jax.ShapeDtypeStruct((m, d), BF16)],
        compiler_params=_params(("parallel",)),
        name="out_proj_ln",
    )(merged, w_all, x, g, b)


def _add_ln_kernel(y_ref, x_ref, g_ref, b_ref, o_ref, ob_ref, *, alpha):
    out = _layer_norm(alpha * x_ref[...] + y_ref[...], g_ref[...], b_ref[...])
    o_ref[...] = out
    ob_ref[...] = out.astype(BF16)


def _add_ln(y, x, g, b, alpha, tm):
    m, d = x.shape
    row = pl.BlockSpec((tm, d), lambda i: (i, 0))
    vec = pl.BlockSpec((1, d), lambda i: (0, 0))
    return pl.pallas_call(
        functools.partial(_add_ln_kernel, alpha=alpha),
        grid=(m // tm,),
        in_specs=[row, row, vec, vec],
        out_specs=[row, row],
        out_shape=[jax.ShapeDtypeStruct((m, d), F32), jax.ShapeDtypeStruct((m, d), BF16)],
        compiler_params=_params(("parallel",)),
        name="peer_add_ln",
    )(y, x, g, b)


LOG2E = 1.4426950408889634


def _top_rows(s, k):
    vals = []
    for _ in range(k):
        mx = jnp.max(s, axis=0, keepdims=True)
        vals.append(mx)
        s = jnp.where(s == mx, NEG_INF, s)
    return vals


def _peer_score_kernel(x_ref, wq_ref, k1_ref, k2_ref, s1_ref, s2_ref, st_ref):
    half = k1_ref.shape[1]
    tm = x_ref.shape[0]
    q = jnp.dot(x_ref[...], wq_ref[0], preferred_element_type=F32)
    s1 = _dot_nt(k1_ref[...], q[:, :half].astype(BF16)) * LOG2E
    s2 = _dot_nt(k2_ref[...], q[:, half:].astype(BF16)) * LOG2E
    for c0 in range(0, tm, LANES):
        a = s1[:, c0:c0 + LANES]
        b = s2[:, c0:c0 + LANES]
        v1 = _top_rows(a, PEER_TOPK + 1)
        v2 = _top_rows(b, PEER_TOPK + 1)
        m1, m2 = v1[0], v2[0]
        v2cat = jnp.concatenate([v - m2 for v in v2[:PEER_TOPK]], axis=0)
        cand = jnp.concatenate([(v - m1) + v2cat for v in v1[:PEER_TOPK]], axis=0)
        top = _top_rows(cand, PEER_TOPK + 1)
        z = jnp.exp2(top[0])
        for kk in range(1, PEER_TOPK):
            z = z + jnp.exp2(top[kk])
        lz = jnp.log(z) * LOG2E
        next_sum = jnp.maximum(top[PEER_TOPK],
                               jnp.maximum(v1[PEER_TOPK] - m1, v2[PEER_TOPK] - m2))
        tau = 0.5 * (top[PEER_TOPK - 1] + next_sum) - lz
        s1_ref[0, :, c0:c0 + LANES] = (a - m1) - lz
        s2_ref[0, :, c0:c0 + LANES] = b - m2
        st_ref[0, :, c0:c0 + LANES] = jnp.broadcast_to(tau, (8, LANES))


def _peer_scores(xb, wq_all, layer, k1, k2, tm):
    n, d = xb.shape
    nk, half = k1.shape
    big = pl.BlockSpec((1, nk, tm), lambda i, h: (h, 0, i))
    return pl.pallas_call(
        _peer_score_kernel,
        grid=(n // tm, N_HEADS),
        in_specs=[pl.BlockSpec((tm, d), lambda i, h: (i, 0)),
                  pl.BlockSpec((1, d, 2 * half), lambda i, h: (layer, 0, h)),
                  pl.BlockSpec(k1.shape, lambda i, h: (0, 0)),
                  pl.BlockSpec(k2.shape, lambda i, h: (0, 0))],
        out_specs=[big, big, pl.BlockSpec((1, 8, tm), lambda i, h: (h, 0, i))],
        out_shape=[jax.ShapeDtypeStruct((N_HEADS, nk, n), F32),
                   jax.ShapeDtypeStruct((N_HEADS, nk, n), F32),
                   jax.ShapeDtypeStruct((N_HEADS, 8, n), F32)],
        compiler_params=_params(("parallel", "arbitrary")),
        name="peer_scores",
    )(xb, wq_all, k1, k2)


PEER_LAG = 2
PEER_TE = 512


def _peer_dense_kernel(x_ref, u_ref, vt_ref, s1_ref, s2_ref, st_ref, y_ref,
                       acc_ref, h_new, h_old, g_new, g_old):
    j = pl.program_id(1)
    n_steps = pl.num_programs(1)
    n_e = n_steps - PEER_LAG
    te = u_ref.shape[1]
    nk = s2_ref.shape[1]

    @pl.when(j == 0)
    def _():
        acc_ref[...] = jnp.zeros_like(acc_ref)
        h_old[...] = jnp.zeros_like(h_old)
        g_old[...] = jnp.zeros_like(g_old)

    acc_ref[...] += jnp.dot(vt_ref[0, 0], g_old[...], preferred_element_type=F32)

    tile = jnp.clip(j - 1, 0, n_e - 1)
    tm = x_ref.shape[0]
    for r in range(te // nk):
        i1 = tile * (te // nk) + r
        rs = slice(r * nk, (r + 1) * nk)
        s1_rows = [s1_ref[h, pl.ds(i1, 1), :] for h in range(N_HEADS)]
        for c0 in range(0, tm, LANES):
            ls = slice(c0, c0 + LANES)
            w = None
            for h in range(N_HEADS):
                arg = s1_rows[h][:, ls] + s2_ref[h, :, ls]
                term = jnp.where(arg >= st_ref[h, 0:1, ls], jnp.exp2(arg), 0.0)
                w = term if w is None else w + term
            g_new[rs, ls] = (_gelu(h_old[rs, ls]) * w).astype(BF16)

    h_new[...] = _dot_nt(u_ref[0], x_ref[...])

    g_old[...] = g_new[...]
    h_old[...] = h_new[...]

    @pl.when(j == n_steps - 1)
    def _():
        y_ref[...] = acc_ref[...].T


def _peer_dense(xb, u_all, vt_all, layer, s1t, s2t, st, tm):
    n_rows, d = xb.shape
    n_e, te = vt_all.shape[1], vt_all.shape[3]
    nk = s1t.shape[1]
    tok3 = lambda i, j: (0, 0, i)
    return pl.pallas_call(
        _peer_dense_kernel,
        grid=(n_rows // tm, n_e + PEER_LAG),
        in_specs=[pl.BlockSpec((tm, d), lambda i, j: (i, 0)),
                  pl.BlockSpec((1, te, d), lambda i, j: (layer, jnp.minimum(j, n_e - 1), 0)),
                  pl.BlockSpec((1, 1, d, te),
                               lambda i, j: (layer, jnp.clip(j - PEER_LAG, 0, n_e - 1), 0, 0)),
                  pl.BlockSpec((N_HEADS, nk, tm), tok3),
                  pl.BlockSpec((N_HEADS, nk, tm), tok3),
                  pl.BlockSpec((N_HEADS, 8, tm), tok3)],
        out_specs=pl.BlockSpec((tm, d), lambda i, j: (i, 0)),
        out_shape=jax.ShapeDtypeStruct((n_rows, d), F32),
        scratch_shapes=[pltpu.VMEM((d, tm), F32),
                        pltpu.VMEM((te, tm), F32), pltpu.VMEM((te, tm), F32),
                        pltpu.VMEM((te, tm), BF16), pltpu.VMEM((te, tm), BF16)],
        compiler_params=_params(("parallel", "arbitrary")),
        name="peer_dense",
    )(xb, u_all, vt_all, s1t, s2t, st)


def _largest_divisor(n, cap, mult):
    best = None
    for t in range(mult, cap + 1, mult):
        if n % t == 0:
            best = t
    assert best is not None, (n, cap, mult)
    return best


def _block_diag(w):
    h, a, b = w.shape
    eye = jnp.eye(h, dtype=w.dtype)
    return (eye[:, None, :, None] * w[:, :, None, :]).reshape(h * a, h * b)


def _channel_mix(l, depth, x, xb, mixed, w, tm_merge, tm):
    alpha = (2 * depth) ** 0.25
    n = x.shape[0]
    merged = _merge(xb, *mixed, w["w_tail"], w["gate_col0"], w["wb"], l, tm_merge, 256)
    x1, x1b = _out_ln(merged, w["wo"], l, x, w["ln1_g"][l], w["ln1_b"][l], alpha,
                      _largest_divisor(n, 256, 16))
    s1t, s2t, st = _peer_scores(x1b, w["wq"], l, w["k1"][l], w["k2"][l], tm)
    y = _peer_dense(x1b, w["u"], w["vt"], l, s1t, s2t, st, tm)
    return _add_ln(y, x1, w["ln2_g"][l], w["ln2_b"][l], alpha, tm)


def kernel(x_prompt, x_sample, cache_k, cache_v, cache_logf, state_conv, state_lru, state_pool, page_table, w_in, conv_w, conv_b, lru_wa, lru_ba, lru_wx, lru_bx, lru_lambda, fox_bf, pool_w, pool_scale, sgu_ln_g, sgu_ln_b, sgu_w, sgu_b, w_branch, w_out, ln1_g, ln1_b, peer_wq, peer_k1, peer_k2, peer_u, peer_v, ln2_g, ln2_b):
    n_seq, seq_len, d = x_prompt.shape
    db, dt, _ = x_sample.shape
    depth = w_in.shape[0]
    c = d // N_BRANCH
    n_p, n_s = n_seq * seq_len, db * dt
    n_pool, page = cache_k.shape[1], cache_k.shape[2]
    n_pages = page_table.shape[1]
    past_len = n_pages * page
    assert c == N_HEADS * HEAD_DIM == N_BRANCH * GROUP_W
    assert seq_len % SGU_CHUNK == 0 and dt <= SGU_CHUNK and dt <= page and page == LANES

    fcol = 4 * c
    rows = lambda v: v.reshape(depth, 1, -1)
    w = dict(
        w_tail=w_in[:, :, fcol + N_HEADS:].astype(BF16),
        gate_col0=3 * c,
        wb=w_branch.astype(BF16), wo=w_out.astype(BF16), wq=peer_wq.astype(BF16),
        k1=peer_k1.astype(BF16), k2=peer_k2.astype(BF16),
        u=peer_u.astype(BF16),
        vt=peer_v.reshape(depth, -1, PEER_TE, d).transpose(0, 1, 3, 2).astype(BF16),
        ln1_g=rows(ln1_g), ln1_b=rows(ln1_b), ln2_g=rows(ln2_g), ln2_b=rows(ln2_b))
    w_head = w_in[:, :, :fcol].astype(BF16)
    w_f = jnp.pad(w_in[:, :, fcol:fcol + N_HEADS], ((0, 0), (0, 0), (0, LANES - N_HEADS))).astype(BF16)
    b_f = jnp.pad(fox_bf, ((0, 0), (0, LANES - N_HEADS))).reshape(depth, 1, LANES)
    wa = jax.vmap(_block_diag)(lru_wa).astype(BF16)
    wx = jax.vmap(_block_diag)(lru_wx).astype(BF16)
    cb, ba, bx, lam = rows(conv_b), rows(lru_ba), rows(lru_bx), rows(lru_lambda)
    pw = pool_w.astype(BF16)
    psc, lng, lnb = rows(pool_scale), rows(sgu_ln_g), rows(sgu_ln_b)

    k_t = cache_k.transpose(0, 1, 3, 4, 2).reshape(depth, n_pool, c, page)
    v_t = cache_v.transpose(0, 1, 3, 4, 2).reshape(depth, n_pool, c, page)
    lf_rows = cache_logf.transpose(0, 1, 3, 2).reshape(depth * n_pool * N_HEADS, page)
    pfx, tot = _page_prefix(lf_rows, _largest_divisor(lf_rows.shape[0], 2048, 8))
    pfx = pfx.reshape(depth, n_pool, N_HEADS, page)
    tot = tot.reshape(depth, n_pool, N_HEADS, page)

    tm_p = _largest_divisor(n_p, 1024, 16)
    tt = _largest_divisor(seq_len, 512, 8)
    n_grp = _largest_divisor(n_pages, 8, 1)
    hd = (N_HEADS, HEAD_DIM)
    tmaj = lambda a: a.reshape(db, dt, -1).transpose(1, 0, 2)
    bmaj = lambda a: a.transpose(1, 0, 2).reshape(n_s, -1)
    pad_lanes = lambda a: jnp.pad(a, ((0, 0), (0, 0), (0, LANES - a.shape[2])))

    xp, xs = x_prompt.reshape(n_p, d), x_sample.reshape(n_s, d)
    xpb, xsb = xp.astype(BF16), xs.astype(BF16)
    p_states, s_states = [], []
    for l in range(depth):
        ph = _matmul(xpb, w_head, l, fcol, tm_p, 512)
        pt = _matmul(xpb, w["w_tail"], l, 3 * c, tm_p, 512)
        sh = _matmul(xsb, w_head, l, fcol, n_s, 512)
        st = _matmul(xsb, w["w_tail"], l, 3 * c, n_s, 512)
        lf_p, c_p = _logf(xpb, w_f[l], b_f[l], n_seq, seq_len, tt)
        lf_s, _ = _logf(xsb, w_f[l], b_f[l], 1, n_s, n_s)
        c_t = c_p[:, :N_HEADS].reshape(n_seq, seq_len, N_HEADS).transpose(0, 2, 1)

        oa_p = _lru_prompt(ph, 0, n_seq, seq_len, tt, conv_w[l], cb[l], wa[l], ba[l], wx[l], bx[l],
                           lam[l])
        oc_p = _pool_prompt(pt, 0, n_seq, seq_len, tt, pw[l], psc[l])
        od_p = _sgu_prompt(pt, 1, 2, n_p, tt, lng[l], lnb[l], sgu_w[l], sgu_b[l].T)
        ob_p = _fox_prompt(ph, c_p, c_t, n_seq, seq_len, tt, tt,
                           c // LANES, 2 * c // LANES, 3 * c // LANES)

        sw = jnp.repeat(sgu_w[l, :, :dt, :dt].transpose(1, 2, 0), GROUP_W, axis=2).reshape(dt, dt, 1, c)
        sb = jnp.repeat(sgu_b[l, :, :dt].T, GROUP_W, axis=1).reshape(dt, 1, c)
        oa_s, oc_s, vn_s, od_s = _sample_mix(
            tmaj(sh[:, :c]), state_conv[l].transpose(1, 0, 2), state_lru[l],
            tmaj(st[:, :c]), state_pool[l].transpose(1, 0, 2),
            tmaj(st[:, c:2 * c]), tmaj(st[:, 2 * c:]),
            conv_w[l], cb[l], wa[l], ba[l], wx[l], bx[l], lam[l], pw[l], psc[l], lng[l], lnb[l],
            sw, sb, past_len)
        new_t = lambda a: pad_lanes(a.reshape(db, dt, -1).transpose(0, 2, 1))
        ob_s = _fox_sample(page_table, l, sh[:, c:2 * c].reshape(db, dt, c), k_t, v_t, pfx, tot,
                           new_t(sh[:, 2 * c:3 * c]), new_t(sh[:, 3 * c:]),
                           new_t(lf_s[:, :N_HEADS]), dt, n_grp).reshape(n_s, c)

        p3 = lambda a: a.reshape(n_seq, seq_len, -1)
        p_states.append((
            ph[:, 2 * c:3 * c].reshape(n_seq, seq_len, *hd),
            ph[:, 3 * c:].reshape(n_seq, seq_len, *hd),
            lf_p[:, :N_HEADS].reshape(n_seq, seq_len, N_HEADS),
            p3(ph)[:, seq_len - (CONV_W - 1):, :c],
            p3(oa_p)[:, -1],
            p3(pt)[:, seq_len - POOL_HIST:, :c]))
        s3 = lambda a: a.reshape(db, dt, -1)
        s_states.append((
            sh[:, 2 * c:3 * c].reshape(db, dt, *hd),
            sh[:, 3 * c:].reshape(db, dt, *hd),
            lf_s[:, :N_HEADS].reshape(db, dt, N_HEADS),
            jnp.concatenate([state_conv[l], s3(sh)[:, :, :c]], axis=1)[:, -(CONV_W - 1):],
            oa_s[dt - 1],
            jnp.concatenate([state_pool[l], s3(st)[:, :, :c]], axis=1)[:, -POOL_HIST:],
            vn_s.transpose(1, 0, 2)))

        xp, xpb = _channel_mix(l, depth, xp, xpb, (oa_p, ob_p, oc_p, od_p), w, tm_p, 512)
        xs, xsb = _channel_mix(l, depth, xs, xsb,
                               (bmaj(oa_s), ob_s, bmaj(oc_s), bmaj(od_s)), w, n_s, n_s)

    yp = xp.reshape(n_seq, seq_len, d)
    ys = xs.reshape(db, dt, d)
    p_out = [jnp.stack([s[j] for s in p_states]) for j in range(6)]
    s_out = [jnp.stack([s[j] for s in s_states]) for j in range(7)]
    return (yp, ys, *p_out, *s_out)
```

```python
import functools

import jax
import jax.numpy as jnp
from jax import lax
from jax.experimental import pallas as pl
from jax.experimental.pallas import tpu as pltpu

F32 = jnp.float32
BF16 = jnp.bfloat16

N_BRANCH = 4
N_HEADS = 8
HEAD_DIM = 64
CONV_W = 4
LRU_C = 8.0
POOL_WINDOWS = (2, 4, 8, 16)
POOL_HIST = 15
GROUP_W = 128
SGU_CHUNK = 128
PEER_NKEYS = 128
PEER_TOPK = 16
LN_EPS = 1e-5
NEG_INF = float("-inf")

VMEM_LIMIT_V7X = 56 * 1024 * 1024
LANES = 128


def _params(semantics):
    return pltpu.CompilerParams(dimension_semantics=semantics, vmem_limit_bytes=VMEM_LIMIT_V7X)


def _gelu(x):
    return 0.5 * x * (1.0 + lax.erf(x * 0.7071067811865476))


def _softplus(z):
    return jnp.maximum(z, 0.0) + jnp.log1p(jnp.exp(-jnp.abs(z)))


def _log_sigmoid(z):
    return -_softplus(-z)


def _expm1(y):
    e = jnp.exp(y)
    return jnp.where(e == 1.0, y, (e - 1.0) * y / jnp.log(e))


def _layer_norm(x, g, b):
    mu = jnp.mean(x, axis=-1, keepdims=True)
    d = x - mu
    var = jnp.mean(d * d, axis=-1, keepdims=True)
    return d * lax.rsqrt(var + LN_EPS) * g + b


def _cumsum_rows(x):
    n = x.shape[0]
    rows = lax.broadcasted_iota(jnp.int32, x.shape, 0)
    s = 1
    while s < n:
        x = x + jnp.where(rows >= s, pltpu.roll(x, s, axis=0), 0.0)
        s *= 2
    return x


def _dot_nt(a, b):
    return lax.dot_general(a, b, (((1,), (1,)), ((), ())), preferred_element_type=F32)


def _matmul_kernel(x_ref, w_ref, o_ref):
    o_ref[...] = _dot_nt(x_ref[...], w_ref[0].astype(BF16))


def _matmul(xb, wt_all, layer, n_cols, tm, tn):
    m, k = xb.shape
    return pl.pallas_call(
        _matmul_kernel,
        grid=(m // tm, n_cols // tn),
        in_specs=[pl.BlockSpec((tm, k), lambda i, j: (i, 0)),
                  pl.BlockSpec((1, tn, k), lambda i, j: (layer, j, 0))],
        out_specs=pl.BlockSpec((tm, tn), lambda i, j: (i, j)),
        out_shape=jax.ShapeDtypeStruct((m, n_cols), F32),
        compiler_params=_params(("parallel", "arbitrary")),
        name="in_proj",
    )(xb, wt_all)


def _logf_kernel(x_ref, w_ref, b_ref, lf_ref, c_ref, carry_ref):
    i = pl.program_id(1)
    fl = _dot_nt(x_ref[...], w_ref[...]) + b_ref[...]
    lf = _log_sigmoid(fl)
    lf_ref[...] = lf

    @pl.when(i == 0)
    def _():
        carry_ref[...] = jnp.zeros_like(carry_ref)

    c = _cumsum_rows(lf) + carry_ref[...]
    c_ref[...] = c
    carry_ref[...] = c[c.shape[0] - 1:, :]


def _logf(xb, w_f, b_f, n_seq, seq_len, tt):
    k = xb.shape[1]
    nt = seq_len // tt
    rows = n_seq * seq_len
    return pl.pallas_call(
        _logf_kernel,
        grid=(n_seq, nt),
        in_specs=[pl.BlockSpec((tt, k), lambda b, i: (b * nt + i, 0)),
                  pl.BlockSpec((LANES, k), lambda b, i: (0, 0)),
                  pl.BlockSpec((1, LANES), lambda b, i: (0, 0))],
        out_specs=[pl.BlockSpec((tt, LANES), lambda b, i: (b * nt + i, 0)),
                   pl.BlockSpec((tt, LANES), lambda b, i: (b * nt + i, 0))],
        out_shape=[jax.ShapeDtypeStruct((rows, LANES), F32),
                   jax.ShapeDtypeStruct((rows, LANES), F32)],
        scratch_shapes=[pltpu.VMEM((1, LANES), F32)],
        compiler_params=_params(("parallel", "arbitrary")),
        name="log_forget",
    )(xb, w_f, b_f)


HALO = 8


def _lru_gates(xc, wa, ba, wx, bx, lam):
    xcb = xc.astype(BF16)
    r = jax.nn.sigmoid(jnp.dot(xcb, wa, preferred_element_type=F32) + ba)
    ig = jax.nn.sigmoid(jnp.dot(xcb, wx, preferred_element_type=F32) + bx)
    log_a = -LRU_C * r * _softplus(-lam)
    a = jnp.exp(log_a)
    u = jnp.sqrt(-_expm1(2.0 * log_a)) * (ig * xc)
    return a, u


def _lru_prompt_kernel(xa_ref, cw_ref, cb_ref, wa_ref, ba_ref, wx_ref, bx_ref, lam_ref,
                       o_ref, ext_ref, h_ref, a_ref, u_ref):
    i = pl.program_id(1)
    tt = xa_ref.shape[0]

    @pl.when(i == 0)
    def _():
        ext_ref[0:HALO, :] = jnp.zeros((HALO, ext_ref.shape[1]), F32)
        h_ref[...] = jnp.zeros_like(h_ref)

    @pl.when(i > 0)
    def _():
        ext_ref[0:HALO, :] = ext_ref[tt:tt + HALO, :]

    ext_ref[HALO:HALO + tt, :] = xa_ref[...]
    cw = cw_ref[...]
    xc = cb_ref[...]
    for j in range(CONV_W):
        off = HALO - (CONV_W - 1) + j
        xc = xc + cw[j:j + 1, :] * ext_ref[off:off + tt, :]
    a, u = _lru_gates(xc, wa_ref[...], ba_ref[...], wx_ref[...], bx_ref[...], lam_ref[...])
    a_ref[...] = a
    u_ref[...] = u

    def body(t, h):
        h = a_ref[pl.ds(t, 1), :] * h + u_ref[pl.ds(t, 1), :]
        o_ref[pl.ds(t, 1), :] = h
        return h

    h_ref[...] = lax.fori_loop(0, tt, body, h_ref[...])


def _lru_prompt(proj, col_blk, n_seq, seq_len, tt, cw, cb, wa, ba, wx, bx, lam):
    c = cw.shape[1]
    nt = seq_len // tt
    const = lambda b, i: (0, 0)
    return pl.pallas_call(
        _lru_prompt_kernel,
        grid=(n_seq, nt),
        in_specs=[pl.BlockSpec((tt, c), lambda b, i: (b * nt + i, col_blk)),
                  pl.BlockSpec((CONV_W, c), const), pl.BlockSpec((1, c), const),
                  pl.BlockSpec((c, c), const), pl.BlockSpec((1, c), const),
                  pl.BlockSpec((c, c), const), pl.BlockSpec((1, c), const),
                  pl.BlockSpec((1, c), const)],
        out_specs=pl.BlockSpec((tt, c), lambda b, i: (b * nt + i, 0)),
        out_shape=jax.ShapeDtypeStruct((n_seq * seq_len, c), F32),
        scratch_shapes=[pltpu.VMEM((HALO + tt, c), F32), pltpu.VMEM((1, c), F32),
                        pltpu.VMEM((tt, c), F32), pltpu.VMEM((tt, c), F32)],
        compiler_params=_params(("parallel", "arbitrary")),
        name="lru_prompt",
    )(proj, cw, cb, wa, ba, wx, bx, lam)


POOL_HALO = 16


def _pool_prompt_kernel(x_ref, w_ref, sc_ref, o_ref, ext_ref):
    i = pl.program_id(1)
    tt = x_ref.shape[0]

    @pl.when(i == 0)
    def _():
        ext_ref[0:POOL_HALO, :] = jnp.zeros((POOL_HALO, ext_ref.shape[1]), F32)

    @pl.when(i > 0)
    def _():
        ext_ref[0:POOL_HALO, :] = ext_ref[tt:tt + POOL_HALO, :]

    ext_ref[POOL_HALO:POOL_HALO + tt, :] = x_ref[...]
    pos = i * tt + lax.broadcasted_iota(jnp.int32, (tt, GROUP_W), 0)
    for g, win in enumerate(POOL_WINDOWS):
        lo, hi = g * GROUP_W, (g + 1) * GROUP_W
        wsum = ext_ref[POOL_HALO:POOL_HALO + tt, lo:hi]
        for j in range(1, win):
            wsum = wsum + ext_ref[POOL_HALO - j:POOL_HALO - j + tt, lo:hi]
        cnt = jnp.minimum(win, pos + 1).astype(F32)
        pooled = wsum / cnt - x_ref[:, lo:hi]
        y = jnp.dot(pooled.astype(BF16), w_ref[g], preferred_element_type=F32)
        o_ref[:, lo:hi] = y * sc_ref[:, lo:hi]


def _pool_prompt(proj, col_blk, n_seq, seq_len, tt, w, scale):
    c = scale.shape[1]
    nt = seq_len // tt
    return pl.pallas_call(
        _pool_prompt_kernel,
        grid=(n_seq, nt),
        in_specs=[pl.BlockSpec((tt, c), lambda b, i: (b * nt + i, col_blk)),
                  pl.BlockSpec(w.shape, lambda b, i: (0, 0, 0)),
                  pl.BlockSpec((1, c), lambda b, i: (0, 0))],
        out_specs=pl.BlockSpec((tt, c), lambda b, i: (b * nt + i, 0)),
        out_shape=jax.ShapeDtypeStruct((n_seq * seq_len, c), F32),
        scratch_shapes=[pltpu.VMEM((POOL_HALO + tt, c), F32)],
        compiler_params=_params(("parallel", "arbitrary")),
        name="pool_prompt",
    )(proj, w, scale)


def _sgu_prompt_kernel(du_ref, dv_ref, g_ref, b_ref, w_ref, sbt_ref, o_ref):
    tt = du_ref.shape[0]
    vn = _layer_norm(_gelu(dv_ref[...]), g_ref[...], b_ref[...])
    vnb = vn.astype(BF16)
    gu = _gelu(du_ref[...])
    row = lax.broadcasted_iota(jnp.int32, (SGU_CHUNK, SGU_CHUNK), 0)
    col = lax.broadcasted_iota(jnp.int32, (SGU_CHUNK, SGU_CHUNK), 1)
    sbt = sbt_ref[...]
    for g in range(N_BRANCH):
        lo, hi = g * GROUP_W, (g + 1) * GROUP_W
        wt = jnp.where(col <= row, w_ref[g], 0.0).astype(BF16)
        bcol = sbt[:, g:g + 1]
        for ch in range(tt // SGU_CHUNK):
            r0, r1 = ch * SGU_CHUNK, (ch + 1) * SGU_CHUNK
            mixed = jnp.dot(wt, vnb[r0:r1, lo:hi], preferred_element_type=F32) + bcol
            o_ref[r0:r1, lo:hi] = gu[r0:r1, lo:hi] * mixed


def _sgu_prompt(proj, col_u, col_v, n_rows, tt, ln_g, ln_b, w, sbt):
    c = ln_g.shape[1]
    return pl.pallas_call(
        _sgu_prompt_kernel,
        grid=(n_rows // tt,),
        in_specs=[pl.BlockSpec((tt, c), lambda i: (i, col_u)),
                  pl.BlockSpec((tt, c), lambda i: (i, col_v)),
                  pl.BlockSpec((1, c), lambda i: (0, 0)),
                  pl.BlockSpec((1, c), lambda i: (0, 0)),
                  pl.BlockSpec(w.shape, lambda i: (0, 0, 0)),
                  pl.BlockSpec(sbt.shape, lambda i: (0, 0))],
        out_specs=pl.BlockSpec((tt, c), lambda i: (i, 0)),
        out_shape=jax.ShapeDtypeStruct((n_rows, c), F32),
        compiler_params=_params(("parallel",)),
        name="sgu_prompt",
    )(proj, proj, ln_g, ln_b, w, sbt)


def _fox_prompt_kernel(q_ref, k_ref, v_ref, cq_ref, ck_ref, o_ref, kb_ref, vb_ref,
                       m_ref, l_ref, acc_ref, *, scale, tk):
    p = pl.program_id(1)
    i = pl.program_id(2)
    tq = q_ref.shape[0]

    @pl.when(i == 0)
    def _():
        kb_ref[...] = k_ref[...].astype(BF16)
        vb_ref[...] = v_ref[...].astype(BF16)

    lane = lax.broadcasted_iota(jnp.int32, (tq, LANES), 1)
    q = q_ref[...] * scale
    cq_all = cq_ref[...]
    qh, cq = [], []
    for hh in range(2):
        head_lanes = (lane >= hh * HEAD_DIM) & (lane < (hh + 1) * HEAD_DIM)
        qh.append(jnp.where(head_lanes, q, 0.0).astype(BF16))
        cq.append(jnp.sum(jnp.where(lane == 2 * p + hh, cq_all, 0.0), axis=1, keepdims=True))
    m_ref[...] = jnp.full(m_ref.shape, NEG_INF, F32)
    l_ref[...] = jnp.zeros_like(l_ref)
    acc_ref[...] = jnp.zeros_like(acc_ref)

    def chunk(j, on_diagonal):
        k0 = pl.multiple_of(j * tk, tk)
        kk = kb_ref[pl.ds(k0, tk), :]
        vv = vb_ref[pl.ds(k0, tk), :]
        for hh in range(2):
            s = _dot_nt(qh[hh], kk)
            ck = ck_ref[0, pl.ds(2 * p + hh, 1), pl.ds(k0, tk)]
            s = s + (cq[hh] - ck)
            if on_diagonal:
                qrow = lax.broadcasted_iota(jnp.int32, (tq, tk), 0)
                kcol = lax.broadcasted_iota(jnp.int32, (tq, tk), 1)
                s = jnp.where(kcol <= qrow, s, NEG_INF)
            m_old = m_ref[hh]
            m_new = jnp.maximum(m_old, jnp.max(s, axis=1, keepdims=True))
            alpha = jnp.exp(m_old - m_new)
            pr = jnp.concatenate([jnp.exp(s[:, c0:c0 + LANES] - m_new)
                                  for c0 in range(0, tk, LANES)], axis=1)
            l_ref[hh] = alpha * l_ref[hh] + jnp.sum(pr, axis=1, keepdims=True)
            acc_ref[hh] = alpha * acc_ref[hh] + jnp.dot(pr.astype(BF16), vv,
                                                        preferred_element_type=F32)
            m_ref[hh] = m_new

    def body(j, carry):
        chunk(j, False)
        return carry

    lax.fori_loop(0, i, body, 0)
    chunk(i, True)
    o_ref[...] = jnp.where(lane < HEAD_DIM, acc_ref[0] / l_ref[0], acc_ref[1] / l_ref[1])


def _fox_prompt(proj, c_rows, c_t, n_seq, seq_len, tq, tk, col_q, col_k, col_v):
    assert tq == tk
    nq = seq_len // tq
    n_pairs = N_HEADS // 2
    kern = functools.partial(_fox_prompt_kernel, scale=HEAD_DIM ** -0.5, tk=tk)
    return pl.pallas_call(
        kern,
        grid=(n_seq, n_pairs, nq),
        in_specs=[pl.BlockSpec((tq, LANES), lambda b, p, i: (b * nq + i, col_q + p)),
                  pl.BlockSpec((seq_len, LANES), lambda b, p, i: (b, col_k + p)),
                  pl.BlockSpec((seq_len, LANES), lambda b, p, i: (b, col_v + p)),
                  pl.BlockSpec((tq, LANES), lambda b, p, i: (b * nq + i, 0)),
                  pl.BlockSpec((1, N_HEADS, seq_len), lambda b, p, i: (b, 0, 0))],
        out_specs=pl.BlockSpec((tq, LANES), lambda b, p, i: (b * nq + i, p)),
        out_shape=jax.ShapeDtypeStruct((n_seq * seq_len, n_pairs * LANES), F32),
        scratch_shapes=[pltpu.VMEM((seq_len, LANES), BF16), pltpu.VMEM((seq_len, LANES), BF16),
                        pltpu.VMEM((2, tq, LANES), F32), pltpu.VMEM((2, tq, LANES), F32),
                        pltpu.VMEM((2, tq, LANES), F32)],
        compiler_params=_params(("parallel", "parallel", "arbitrary")),
        name="fox_prompt",
    )(proj, proj, proj, c_rows, c_t)


def _lane_prefix(x):
    n = x.shape[1]
    lane = lax.broadcasted_iota(jnp.int32, x.shape, 1)
    s = 1
    while s < n:
        x = x + jnp.where(lane >= s, pltpu.roll(x, s, axis=1), 0.0)
        s *= 2
    return x


def _page_prefix_kernel(lf_ref, p_ref, tot_ref):
    lf = lf_ref[...]
    p_ref[...] = _lane_prefix(lf)
    tot_ref[...] = jnp.broadcast_to(jnp.sum(lf, axis=1, keepdims=True), lf.shape)


def _page_prefix(lf_rows, tr):
    rows, page = lf_rows.shape
    spec = pl.BlockSpec((tr, page), lambda i: (i, 0))
    shp = jax.ShapeDtypeStruct((rows, page), F32)
    return pl.pallas_call(
        _page_prefix_kernel,
        grid=(rows // tr,),
        in_specs=[spec], out_specs=[spec, spec], out_shape=[shp, shp],
        compiler_params=_params(("parallel",)),
        name="page_prefix",
    )(lf_rows)


def _fox_sample_kernel(pt_ref, q_ref, *refs, scale, n_new, n_grp):
    k_refs = refs[0:n_grp]
    v_refs = refs[n_grp:2 * n_grp]
    p_refs = refs[2 * n_grp:3 * n_grp]
    t_refs = refs[3 * n_grp:4 * n_grp]
    kn_ref, vn_ref, lfn_ref, o_ref, qb_ref, m_ref, l_ref, acc_ref, cb_ref = refs[4 * n_grp:]
    j = pl.program_id(1)
    n_steps = pl.num_programs(1)
    n_tok, c = q_ref.shape[1], q_ref.shape[2]
    rows = n_tok * N_HEADS
    row_i = lax.broadcasted_iota(jnp.int32, (N_HEADS, c), 0)
    lane_i = lax.broadcasted_iota(jnp.int32, (N_HEADS, c), 1)
    own_lanes = (lane_i >= row_i * HEAD_DIM) & (lane_i < (row_i + 1) * HEAD_DIM)

    @pl.when(j == 0)
    def _():
        q = q_ref[0] * scale
        for t in range(n_tok):
            qt = jnp.broadcast_to(q[t:t + 1, :], (N_HEADS, c))
            qb_ref[t * N_HEADS:(t + 1) * N_HEADS, :] = jnp.where(own_lanes, qt, 0.0).astype(BF16)
        m_ref[...] = jnp.full(m_ref.shape, NEG_INF, F32)
        l_ref[...] = jnp.zeros_like(l_ref)
        acc_ref[...] = jnp.zeros_like(acc_ref)
        cb_ref[...] = jnp.zeros_like(cb_ref)

    qb = qb_ref[...]

    def update(s_list, v_list):
        m_old = m_ref[...]
        m_new = m_old
        for s in s_list:
            m_new = jnp.maximum(m_new, jnp.max(s, axis=1, keepdims=True))
        alpha = jnp.exp(m_old - m_new)
        l_new = alpha * l_ref[...]
        acc = alpha * acc_ref[...]
        for s, vv in zip(s_list, v_list):
            pr = jnp.exp(s - m_new)
            l_new = l_new + jnp.sum(pr, axis=1, keepdims=True)
            acc = acc + _dot_nt(pr.astype(BF16), vv.astype(BF16))
        l_ref[...] = l_new
        acc_ref[...] = acc
        m_ref[...] = m_new

    base = cb_ref[...]
    s_list, v_list = [], []
    for g in range(n_grp):
        c_k = base + p_refs[g][0, 0]
        base = base + t_refs[g][0, 0]
        s = jnp.dot(qb, k_refs[g][0, 0].astype(BF16), preferred_element_type=F32)
        s_list.append(s - jnp.concatenate([c_k] * n_tok, axis=0))
        v_list.append(v_refs[g][0, 0])
    cb_ref[...] = base
    update(s_list, v_list)

    @pl.when(j == n_steps - 1)
    def _():
        nk_new = kn_ref.shape[2]
        c_new = cb_ref[:, 0:nk_new] + _lane_prefix(lfn_ref[0])
        s = jnp.dot(qb, kn_ref[0].astype(BF16), preferred_element_type=F32)
        s = s - jnp.concatenate([c_new] * n_tok, axis=0)
        r_n = lax.broadcasted_iota(jnp.int32, (rows, nk_new), 0)
        c_n = lax.broadcasted_iota(jnp.int32, (rows, nk_new), 1)
        ok = (c_n * N_HEADS <= (r_n | (N_HEADS - 1))) & (c_n < n_new)
        update([jnp.where(ok, s, NEG_INF)], [vn_ref[0]])
        out = acc_ref[...] / l_ref[...]
        for t in range(n_tok):
            blk = out[t * N_HEADS:(t + 1) * N_HEADS, :]
            o_ref[0, t:t + 1, :] = jnp.sum(jnp.where(own_lanes, blk, 0.0), axis=0, keepdims=True)


def _fox_sample(page_table, layer, q, k_t, v_t, pfx, tot, k_new, v_new, lf_new, n_new, n_grp):
    db, dt, c = q.shape
    n_pages = page_table.shape[1]
    page = k_t.shape[3]
    nkn = k_new.shape[2]
    rows = dt * N_HEADS
    kern = functools.partial(_fox_sample_kernel, scale=HEAD_DIM ** -0.5, n_new=n_new, n_grp=n_grp)

    def paged(shape2, g):
        return pl.BlockSpec((1, 1) + shape2,
                            lambda b, j, pt: (layer, pt[b, j * n_grp + g], 0, 0))

    per_b = lambda shape2: pl.BlockSpec((1,) + shape2, lambda b, j, pt: (b, 0, 0))
    grid_spec = pltpu.PrefetchScalarGridSpec(
        num_scalar_prefetch=1,
        grid=(db, n_pages // n_grp),
        in_specs=[per_b((dt, c))]
        + [paged((c, page), g) for g in range(n_grp)]
        + [paged((c, page), g) for g in range(n_grp)]
        + [paged((N_HEADS, page), g) for g in range(n_grp)]
        + [paged((N_HEADS, page), g) for g in range(n_grp)]
        + [per_b((c, nkn)), per_b((c, nkn)), per_b((N_HEADS, nkn))],
        out_specs=per_b((dt, c)),
        scratch_shapes=[pltpu.VMEM((rows, c), BF16), pltpu.VMEM((rows, 1), F32),
                        pltpu.VMEM((rows, 1), F32), pltpu.VMEM((rows, c), F32),
                        pltpu.VMEM((N_HEADS, page), F32)],
    )
    return pl.pallas_call(
        kern,
        grid_spec=grid_spec,
        out_shape=jax.ShapeDtypeStruct((db, dt, c), F32),
        compiler_params=_params(("parallel", "arbitrary")),
        name="fox_sample",
    )(page_table, q, *([k_t] * n_grp), *([v_t] * n_grp), *([pfx] * n_grp), *([tot] * n_grp),
      k_new, v_new, lf_new)


def _sample_mix_kernel(xa_ref, sconv_ref, h0_ref, xc_ref, spool_ref, du_ref, dv_ref,
                       cw_ref, cb_ref, wa_ref, ba_ref, wx_ref, bx_ref, lam_ref,
                       pw_ref, psc_ref, lng_ref, lnb_ref, sw_ref, sb_ref,
                       oa_ref, oc_ref, vn_ref, od_ref, *, pos0):
    nt = xa_ref.shape[0]
    cw = cw_ref[...]
    ext = [sconv_ref[j] for j in range(CONV_W - 1)] + [xa_ref[t] for t in range(nt)]
    h = h0_ref[...]
    for t in range(nt):
        xc = cb_ref[...]
        for j in range(CONV_W):
            xc = xc + cw[j:j + 1, :] * ext[t + j]
        a, u = _lru_gates(xc, wa_ref[...], ba_ref[...], wx_ref[...], bx_ref[...], lam_ref[...])
        h = a * h + u
        oa_ref[t] = h
    pext = [spool_ref[j] for j in range(POOL_HIST)] + [xc_ref[t] for t in range(nt)]
    for t in range(nt):
        x_t = pext[POOL_HIST + t]
        for g, win in enumerate(POOL_WINDOWS):
            lo, hi = g * GROUP_W, (g + 1) * GROUP_W
            wsum = x_t[:, lo:hi]
            for j in range(1, win):
                wsum = wsum + pext[POOL_HIST + t - j][:, lo:hi]
            cnt = float(min(win, pos0 + t + 1))
            pooled = wsum / cnt - x_t[:, lo:hi]
            y = jnp.dot(pooled.astype(BF16), pw_ref[g], preferred_element_type=F32)
            oc_ref[t, :, lo:hi] = y * psc_ref[:, lo:hi]
    vns = []
    for t in range(nt):
        vn = _layer_norm(_gelu(dv_ref[t]), lng_ref[...], lnb_ref[...])
        vn_ref[t] = vn
        vns.append(vn)
    for t in range(nt):
        mixed = sb_ref[t]
        for s in range(t + 1):
            mixed = mixed + sw_ref[t, s] * vns[s]
        od_ref[t] = _gelu(du_ref[t]) * mixed


def _sample_mix(xa, sconv, h0, xc, spool, du, dv, cw, cb, wa, ba, wx, bx, lam,
                pw, psc, lng, lnb, sw, sb, pos0):
    shp = jax.ShapeDtypeStruct(xa.shape, F32)
    kern = functools.partial(_sample_mix_kernel, pos0=pos0)
    return pl.pallas_call(
        kern,
        out_shape=[shp, shp, shp, shp],
        compiler_params=pltpu.CompilerParams(vmem_limit_bytes=VMEM_LIMIT_V7X),
        name="sample_mix",
    )(xa, sconv, h0, xc, spool, du, dv, cw, cb, wa, ba, wx, bx, lam, pw, psc, lng, lnb, sw, sb)


def _merge_kernel(x_ref, oa_ref, ob_ref, oc_ref, od_ref, g0_ref, g1_ref, g2_ref, g3_ref,
                  wb_ref, o_ref):
    x = x_ref[...]
    acc = None
    for n, (o_n, g_n) in enumerate(((oa_ref, g0_ref), (ob_ref, g1_ref),
                                    (oc_ref, g2_ref), (od_ref, g3_ref))):
        gate = jax.nn.sigmoid(_dot_nt(x, g_n[0]))
        br = jnp.dot(o_n[...].astype(BF16), wb_ref[0, n], preferred_element_type=F32)
        acc = gate * br if acc is None else acc + gate * br
    o_ref[...] = acc.astype(o_ref.dtype)


def _merge(xb, oa, ob, oc, od, wg_all, gate_col0, wb_all, layer, tm, tn):
    m, d = xb.shape
    c = oa.shape[1]
    ncol = d // tn
    blk0 = gate_col0 // tn
    o_spec = pl.BlockSpec((tm, c), lambda i, j: (i, 0))
    g_specs = [pl.BlockSpec((1, tn, d), (lambda n: (lambda i, j: (layer, blk0 + n * ncol + j, 0)))(n))
               for n in range(N_BRANCH)]
    return pl.pallas_call(
        _merge_kernel,
        grid=(m // tm, ncol),
        in_specs=[pl.BlockSpec((tm, d), lambda i, j: (i, 0)), o_spec, o_spec, o_spec, o_spec]
        + g_specs + [pl.BlockSpec((1, N_BRANCH, c, tn), lambda i, j: (layer, 0, 0, j))],
        out_specs=pl.BlockSpec((tm, tn), lambda i, j: (i, j)),
        out_shape=jax.ShapeDtypeStruct((m, d), BF16),
        compiler_params=_params(("parallel", "arbitrary")),
        name="gated_merge",
    )(xb, oa, ob, oc, od, wg_all, wg_all, wg_all, wg_all, wb_all)


def _out_ln_kernel(m_ref, w_ref, x_ref, g_ref, b_ref, o_ref, ob_ref, *, alpha):
    y = jnp.dot(m_ref[...], w_ref[0], preferred_element_type=F32)
    out = _layer_norm(alpha * x_ref[...] + y, g_ref[...], b_ref[...])
    o_ref[...] = out
    ob_ref[...] = out.astype(BF16)


def _out_ln(merged, w_all, layer, x, g, b, alpha, tm):
    m, d = x.shape
    row = pl.BlockSpec((tm, d), lambda i: (i, 0))
    vec = pl.BlockSpec((1, d), lambda i: (0, 0))
    return pl.pallas_call(
        functools.partial(_out_ln_kernel, alpha=alpha),
        grid=(m // tm,),
        in_specs=[row, pl.BlockSpec((1, d, d), lambda i: (layer, 0, 0)), row, vec, vec],
        out_specs=[row, row],
        out_shape=[jax.ShapeDtypeStruct((m, d), F32), jax.ShapeDtypeStruct((m, d), BF16)],
        compiler_params=_params(("parallel",)),
        name="out_proj_ln",
    )(merged, w_all, x, g, b)


def _add_ln_kernel(y_ref, x_ref, g_ref, b_ref, o_ref, ob_ref, *, alpha):
    out = _layer_norm(alpha * x_ref[...] + y_ref[...], g_ref[...], b_ref[...])
    o_ref[...] = out
    ob_ref[...] = out.astype(BF16)


def _add_ln(y, x, g, b, alpha, tm):
    m, d = x.shape
    row = pl.BlockSpec((tm, d), lambda i: (i, 0))
    vec = pl.BlockSpec((1, d), lambda i: (0, 0))
    return pl.pallas_call(
        functools.partial(_add_ln_kernel, alpha=alpha),
        grid=(m // tm,),
        in_specs=[row, row, vec, vec],
        out_specs=[row, row],
        out_shape=[jax.ShapeDtypeStruct((m, d), F32), jax.ShapeDtypeStruct((m, d), BF16)],
        compiler_params=_params(("parallel",)),
        name="peer_add_ln",
    )(y, x, g, b)


LOG2E = 1.4426950408889634


def _top_rows(s, k):
    vals = []
    for _ in range(k):
        mx = jnp.max(s, axis=0, keepdims=True)
        vals.append(mx)
        s = jnp.where(s == mx, NEG_INF, s)
    return vals


def _peer_score_kernel(x_ref, wq_ref, k1_ref, k2_ref, s1_ref, s2_ref, st_ref):
    half = k1_ref.shape[1]
    tm = x_ref.shape[0]
    q = jnp.dot(x_ref[...], wq_ref[0], preferred_element_type=F32)
    s1 = _dot_nt(k1_ref[...], q[:, :half].astype(BF16)) * LOG2E
    s2 = _dot_nt(k2_ref[...], q[:, half:].astype(BF16)) * LOG2E
    for c0 in range(0, tm, LANES):
        a = s1[:, c0:c0 + LANES]
        b = s2[:, c0:c0 + LANES]
        v1 = _top_rows(a, PEER_TOPK + 1)
        v2 = _top_rows(b, PEER_TOPK + 1)
        m1, m2 = v1[0], v2[0]
        v2cat = jnp.concatenate([v - m2 for v in v2[:PEER_TOPK]], axis=0)
        cand = jnp.concatenate([(v - m1) + v2cat for v in v1[:PEER_TOPK]], axis=0)
        top = _top_rows(cand, PEER_TOPK + 1)
        z = jnp.exp2(top[0])
        for kk in range(1, PEER_TOPK):
            z = z + jnp.exp2(top[kk])
        lz = jnp.log(z) * LOG2E
        next_sum = jnp.maximum(top[PEER_TOPK],
                               jnp.maximum(v1[PEER_TOPK] - m1, v2[PEER_TOPK] - m2))
        tau = 0.5 * (top[PEER_TOPK - 1] + next_sum) - lz
        s1_ref[0, :, c0:c0 + LANES] = (a - m1) - lz
        s2_ref[0, :, c0:c0 + LANES] = b - m2
        st_ref[0, :, c0:c0 + LANES] = jnp.broadcast_to(tau, (8, LANES))


def _peer_scores(xb, wq_all, layer, k1, k2, tm):
    n, d = xb.shape
    nk, half = k1.shape
    big = pl.BlockSpec((1, nk, tm), lambda i, h: (h, 0, i))
    return pl.pallas_call(
        _peer_score_kernel,
        grid=(n // tm, N_HEADS),
        in_specs=[pl.BlockSpec((tm, d), lambda i, h: (i, 0)),
                  pl.BlockSpec((1, d, 2 * half), lambda i, h: (layer, 0, h)),
                  pl.BlockSpec(k1.shape, lambda i, h: (0, 0)),
                  pl.BlockSpec(k2.shape, lambda i, h: (0, 0))],
        out_specs=[big, big, pl.BlockSpec((1, 8, tm), lambda i, h: (h, 0, i))],
        out_shape=[jax.ShapeDtypeStruct((N_HEADS, nk, n), F32),
                   jax.ShapeDtypeStruct((N_HEADS, nk, n), F32),
                   jax.ShapeDtypeStruct((N_HEADS, 8, n), F32)],
        compiler_params=_params(("parallel", "arbitrary")),
        name="peer_scores",
    )(xb, wq_all, k1, k2)


PEER_LAG = 2
PEER_TE = 512


def _peer_dense_kernel(x_ref, u_ref, vt_ref, s1_ref, s2_ref, st_ref, y_ref,
                       acc_ref, h_new, h_old, g_new, g_old):
    j = pl.program_id(1)
    n_steps = pl.num_programs(1)
    n_e = n_steps - PEER_LAG
    te = u_ref.shape[1]
    nk = s2_ref.shape[1]

    @pl.when(j == 0)
    def _():
        acc_ref[...] = jnp.zeros_like(acc_ref)
        h_old[...] = jnp.zeros_like(h_old)
        g_old[...] = jnp.zeros_like(g_old)

    acc_ref[...] += jnp.dot(vt_ref[0, 0], g_old[...], preferred_element_type=F32)

    tile = jnp.clip(j - 1, 0, n_e - 1)
    tm = x_ref.shape[0]
    for r in range(te // nk):
        i1 = tile * (te // nk) + r
        rs = slice(r * nk, (r + 1) * nk)
        s1_rows = [s1_ref[h, pl.ds(i1, 1), :] for h in range(N_HEADS)]
        for c0 in range(0, tm, LANES):
            ls = slice(c0, c0 + LANES)
            w = None
            for h in range(N_HEADS):
                arg = s1_rows[h][:, ls] + s2_ref[h, :, ls]
                term = jnp.where(arg >= st_ref[h, 0:1, ls], jnp.exp2(arg), 0.0)
                w = term if w is None else w + term
            g_new[rs, ls] = (_gelu(h_old[rs, ls]) * w).astype(BF16)

    h_new[...] = _dot_nt(u_ref[0], x_ref[...])

    g_old[...] = g_new[...]
    h_old[...] = h_new[...]

    @pl.when(j == n_steps - 1)
    def _():
        y_ref[...] = acc_ref[...].T


def _peer_dense(xb, u_all, vt_all, layer, s1t, s2t, st, tm):
    n_rows, d = xb.shape
    n_e, te = vt_all.shape[1], vt_all.shape[3]
    nk = s1t.shape[1]
    tok3 = lambda i, j: (0, 0, i)
    once = pl.Buffered(1)
    return pl.pallas_call(
        _peer_dense_kernel,
        grid=(n_rows // tm, n_e + PEER_LAG),
        in_specs=[pl.BlockSpec((tm, d), lambda i, j: (i, 0), pipeline_mode=once),
                  pl.BlockSpec((1, te, d), lambda i, j: (layer, jnp.minimum(j, n_e - 1), 0)),
                  pl.BlockSpec((1, 1, d, te),
                               lambda i, j: (layer, jnp.clip(j - PEER_LAG, 0, n_e - 1), 0, 0)),
                  pl.BlockSpec((N_HEADS, nk, tm), tok3, pipeline_mode=once),
                  pl.BlockSpec((N_HEADS, nk, tm), tok3, pipeline_mode=once),
                  pl.BlockSpec((N_HEADS, 8, tm), tok3, pipeline_mode=once)],
        out_specs=pl.BlockSpec((tm, d), lambda i, j: (i, 0)),
        out_shape=jax.ShapeDtypeStruct((n_rows, d), F32),
        scratch_shapes=[pltpu.VMEM((d, tm), F32),
                        pltpu.VMEM((te, tm), F32), pltpu.VMEM((te, tm), F32),
                        pltpu.VMEM((te, tm), BF16), pltpu.VMEM((te, tm), BF16)],
        compiler_params=_params(("parallel", "arbitrary")),
        name="peer_dense",
    )(xb, u_all, vt_all, s1t, s2t, st)


def _largest_divisor(n, cap, mult):
    best = None
    for t in range(mult, cap + 1, mult):
        if n % t == 0:
            best = t
    assert best is not None, (n, cap, mult)
    return best


def _block_diag(w):
    h, a, b = w.shape
    eye = jnp.eye(h, dtype=w.dtype)
    return (eye[:, None, :, None] * w[:, :, None, :]).reshape(h * a, h * b)


def _channel_mix(l, depth, x, xb, mixed, w, tm_merge, tm):
    alpha = (2 * depth) ** 0.25
    n = x.shape[0]
    merged = _merge(xb, *mixed, w["w_tail"], w["gate_col0"], w["wb"], l, tm_merge, 256)
    x1, x1b = _out_ln(merged, w["wo"], l, x, w["ln1_g"][l], w["ln1_b"][l], alpha,
                      _largest_divisor(n, 256, 16))
    s1t, s2t, st = _peer_scores(x1b, w["wq"], l, w["k1"][l], w["k2"][l], tm)
    y = _peer_dense(x1b, w["u"], w["vt"], l, s1t, s2t, st, tm_merge)
    return _add_ln(y, x1, w["ln2_g"][l], w["ln2_b"][l], alpha, tm)


def kernel(x_prompt, x_sample, cache_k, cache_v, cache_logf, state_conv, state_lru, state_pool, page_table, w_in, conv_w, conv_b, lru_wa, lru_ba, lru_wx, lru_bx, lru_lambda, fox_bf, pool_w, pool_scale, sgu_ln_g, sgu_ln_b, sgu_w, sgu_b, w_branch, w_out, ln1_g, ln1_b, peer_wq, peer_k1, peer_k2, peer_u, peer_v, ln2_g, ln2_b):
    n_seq, seq_len, d = x_prompt.shape
    db, dt, _ = x_sample.shape
    depth = w_in.shape[0]
    c = d // N_BRANCH
    n_p, n_s = n_seq * seq_len, db * dt
    n_pool, page = cache_k.shape[1], cache_k.shape[2]
    n_pages = page_table.shape[1]
    past_len = n_pages * page
    assert c == N_HEADS * HEAD_DIM == N_BRANCH * GROUP_W
    assert seq_len % SGU_CHUNK == 0 and dt <= SGU_CHUNK and dt <= page and page == LANES

    fcol = 4 * c
    rows = lambda v: v.reshape(depth, 1, -1)
    w_in_t = w_in.transpose(0, 2, 1)
    w = dict(
        w_tail=w_in_t[:, fcol + N_HEADS:, :].astype(BF16),
        gate_col0=3 * c,
        wb=w_branch.astype(BF16), wo=w_out.astype(BF16), wq=peer_wq.astype(BF16),
        k1=peer_k1.astype(BF16), k2=peer_k2.astype(BF16),
        u=peer_u.astype(BF16),
        vt=peer_v.reshape(depth, -1, PEER_TE, d).transpose(0, 1, 3, 2).astype(BF16),
        ln1_g=rows(ln1_g), ln1_b=rows(ln1_b), ln2_g=rows(ln2_g), ln2_b=rows(ln2_b))
    w_head = w_in_t
    w_f = jnp.pad(w_in_t[:, fcol:fcol + N_HEADS, :], ((0, 0), (0, LANES - N_HEADS), (0, 0))).astype(BF16)
    b_f = jnp.pad(fox_bf, ((0, 0), (0, LANES - N_HEADS))).reshape(depth, 1, LANES)
    wa = jax.vmap(_block_diag)(lru_wa).astype(BF16)
    wx = jax.vmap(_block_diag)(lru_wx).astype(BF16)
    cb, ba, bx, lam = rows(conv_b), rows(lru_ba), rows(lru_bx), rows(lru_lambda)
    pw = pool_w.astype(BF16)
    psc, lng, lnb = rows(pool_scale), rows(sgu_ln_g), rows(sgu_ln_b)

    k_t = cache_k.transpose(0, 1, 3, 4, 2).reshape(depth, n_pool, c, page)
    v_t = cache_v.transpose(0, 1, 3, 4, 2).reshape(depth, n_pool, c, page)
    lf_rows = cache_logf.transpose(0, 1, 3, 2).reshape(depth * n_pool * N_HEADS, page)
    pfx, tot = _page_prefix(lf_rows, _largest_divisor(lf_rows.shape[0], 2048, 8))
    pfx = pfx.reshape(depth, n_pool, N_HEADS, page)
    tot = tot.reshape(depth, n_pool, N_HEADS, page)

    tm_p = _largest_divisor(n_p, 1024, 16)
    tt = _largest_divisor(seq_len, 512, 8)
    n_grp = _largest_divisor(n_pages, 16, 1)
    hd = (N_HEADS, HEAD_DIM)
    tmaj = lambda a: a.reshape(db, dt, -1).transpose(1, 0, 2)
    bmaj = lambda a: a.transpose(1, 0, 2).reshape(n_s, -1)
    pad_lanes = lambda a: jnp.pad(a, ((0, 0), (0, 0), (0, LANES - a.shape[2])))

    xp, xs = x_prompt.reshape(n_p, d), x_sample.reshape(n_s, d)
    xpb, xsb = xp.astype(BF16), xs.astype(BF16)
    p_states, s_states = [], []
    for l in range(depth):
        ph = _matmul(xpb, w_head, l, fcol, tm_p, 512)
        pt = _matmul(xpb, w["w_tail"], l, 3 * c, tm_p, 512)
        sh = _matmul(xsb, w_head, l, fcol, n_s, 512)
        st = _matmul(xsb, w["w_tail"], l, 3 * c, n_s, 512)
        lf_p, c_p = _logf(xpb, w_f[l], b_f[l], n_seq, seq_len, tt)
        lf_s, _ = _logf(xsb, w_f[l], b_f[l], 1, n_s, n_s)
        c_t = c_p[:, :N_HEADS].reshape(n_seq, seq_len, N_HEADS).transpose(0, 2, 1)

        oa_p = _lru_prompt(ph, 0, n_seq, seq_len, tt, conv_w[l], cb[l], wa[l], ba[l], wx[l], bx[l],
                           lam[l])
        oc_p = _pool_prompt(pt, 0, n_seq, seq_len, tt, pw[l], psc[l])
        od_p = _sgu_prompt(pt, 1, 2, n_p, tt, lng[l], lnb[l], sgu_w[l], sgu_b[l].T)
        ob_p = _fox_prompt(ph, c_p, c_t, n_seq, seq_len, tt, tt,
                           c // LANES, 2 * c // LANES, 3 * c // LANES)

        sw = jnp.repeat(sgu_w[l, :, :dt, :dt].transpose(1, 2, 0), GROUP_W, axis=2).reshape(dt, dt, 1, c)
        sb = jnp.repeat(sgu_b[l, :, :dt].T, GROUP_W, axis=1).reshape(dt, 1, c)
        oa_s, oc_s, vn_s, od_s = _sample_mix(
            tmaj(sh[:, :c]), state_conv[l].transpose(1, 0, 2), state_lru[l],
            tmaj(st[:, :c]), state_pool[l].transpose(1, 0, 2),
            tmaj(st[:, c:2 * c]), tmaj(st[:, 2 * c:]),
            conv_w[l], cb[l], wa[l], ba[l], wx[l], bx[l], lam[l], pw[l], psc[l], lng[l], lnb[l],
            sw, sb, past_len)
        new_t = lambda a: pad_lanes(a.reshape(db, dt, -1).transpose(0, 2, 1))
        ob_s = _fox_sample(page_table, l, sh[:, c:2 * c].reshape(db, dt, c), k_t, v_t, pfx, tot,
                           new_t(sh[:, 2 * c:3 * c]), new_t(sh[:, 3 * c:]),
                           new_t(lf_s[:, :N_HEADS]), dt, n_grp).reshape(n_s, c)

        p3 = lambda a: a.reshape(n_seq, seq_len, -1)
        p_states.append((
            ph[:, 2 * c:3 * c].reshape(n_seq, seq_len, *hd),
            ph[:, 3 * c:].reshape(n_seq, seq_len, *hd),
            lf_p[:, :N_HEADS].reshape(n_seq, seq_len, N_HEADS),
            p3(ph)[:, seq_len - (CONV_W - 1):, :c],
            p3(oa_p)[:, -1],
            p3(pt)[:, seq_len - POOL_HIST:, :c]))
        s3 = lambda a: a.reshape(db, dt, -1)
        s_states.append((
            sh[:, 2 * c:3 * c].reshape(db, dt, *hd),
            sh[:, 3 * c:].reshape(db, dt, *hd),
            lf_s[:, :N_HEADS].reshape(db, dt, N_HEADS),
            jnp.concatenate([state_conv[l], s3(sh)[:, :, :c]], axis=1)[:, -(CONV_W - 1):],
            oa_s[dt - 1],
            jnp.concatenate([state_pool[l], s3(st)[:, :, :c]], axis=1)[:, -POOL_HIST:],
            vn_s.transpose(1, 0, 2)))

        xp, xpb = _channel_mix(l, depth, xp, xpb, (oa_p, ob_p, oc_p, od_p), w, tm_p, 512)
        xs, xsb = _channel_mix(l, depth, xs, xsb,
                               (bmaj(oa_s), ob_s, bmaj(oc_s), bmaj(od_s)), w, n_s, n_s)

    yp = xp.reshape(n_seq, seq_len, d)
    ys = xs.reshape(db, dt, d)
    p_out = [jnp.stack([s[j] for s in p_states]) for j in range(6)]
    s_out = [jnp.stack([s[j] for s in s_states]) for j in range(7)]
    return (yp, ys, *p_out, *s_out)
```

```python
import functools

import jax
import jax.numpy as jnp
from jax import lax
from jax.experimental import pallas as pl
from jax.experimental.pallas import tpu as pltpu

F32 = jnp.float32
BF16 = jnp.bfloat16

N_BRANCH = 4
N_HEADS = 8
HEAD_DIM = 64
CONV_W = 4
LRU_C = 8.0
POOL_WINDOWS = (2, 4, 8, 16)
POOL_HIST = 15
GROUP_W = 128
SGU_CHUNK = 128
PEER_NKEYS = 128
PEER_TOPK = 16
LN_EPS = 1e-5
NEG_INF = float("-inf")

VMEM_LIMIT_V7X = 56 * 1024 * 1024
LANES = 128


def _params(semantics):
    return pltpu.CompilerParams(dimension_semantics=semantics, vmem_limit_bytes=VMEM_LIMIT_V7X)


def _gelu(x):
    return 0.5 * x * (1.0 + lax.erf(x * 0.7071067811865476))


def _softplus(z):
    return jnp.maximum(z, 0.0) + jnp.log1p(jnp.exp(-jnp.abs(z)))


def _log_sigmoid(z):
    return -_softplus(-z)


def _expm1(y):
    e = jnp.exp(y)
    return jnp.where(e == 1.0, y, (e - 1.0) * y / jnp.log(e))


def _layer_norm(x, g, b):
    mu = jnp.mean(x, axis=-1, keepdims=True)
    d = x - mu
    var = jnp.mean(d * d, axis=-1, keepdims=True)
    return d * lax.rsqrt(var + LN_EPS) * g + b


def _cumsum_rows(x):
    n = x.shape[0]
    rows = lax.broadcasted_iota(jnp.int32, x.shape, 0)
    s = 1
    while s < n:
        x = x + jnp.where(rows >= s, pltpu.roll(x, s, axis=0), 0.0)
        s *= 2
    return x


def _dot_nt(a, b):
    return lax.dot_general(a, b, (((1,), (1,)), ((), ())), preferred_element_type=F32)


def _matmul_kernel(x_ref, w_ref, o_ref):
    o_ref[...] = _dot_nt(x_ref[...], w_ref[0].astype(BF16))


def _matmul(xb, wt_all, layer, n_cols, tm, tn):
    m, k = xb.shape
    return pl.pallas_call(
        _matmul_kernel,
        grid=(m // tm, n_cols // tn),
        in_specs=[pl.BlockSpec((tm, k), lambda i, j: (i, 0)),
                  pl.BlockSpec((1, tn, k), lambda i, j: (layer, j, 0))],
        out_specs=pl.BlockSpec((tm, tn), lambda i, j: (i, j)),
        out_shape=jax.ShapeDtypeStruct((m, n_cols), F32),
        compiler_params=_params(("parallel", "arbitrary")),
        name="in_proj",
    )(xb, wt_all)


def _logf_kernel(x_ref, w_ref, b_ref, lf_ref, c_ref, carry_ref):
    i = pl.program_id(1)
    fl = _dot_nt(x_ref[...], w_ref[...]) + b_ref[...]
    lf = _log_sigmoid(fl)
    lf_ref[...] = lf

    @pl.when(i == 0)
    def _():
        carry_ref[...] = jnp.zeros_like(carry_ref)

    c = _cumsum_rows(lf) + carry_ref[...]
    c_ref[...] = c
    carry_ref[...] = c[c.shape[0] - 1:, :]


def _logf(xb, w_f, b_f, n_seq, seq_len, tt):
    k = xb.shape[1]
    nt = seq_len // tt
    rows = n_seq * seq_len
    return pl.pallas_call(
        _logf_kernel,
        grid=(n_seq, nt),
        in_specs=[pl.BlockSpec((tt, k), lambda b, i: (b * nt + i, 0)),
                  pl.BlockSpec((LANES, k), lambda b, i: (0, 0)),
                  pl.BlockSpec((1, LANES), lambda b, i: (0, 0))],
        out_specs=[pl.BlockSpec((tt, LANES), lambda b, i: (b * nt + i, 0)),
                   pl.BlockSpec((tt, LANES), lambda b, i: (b * nt + i, 0))],
        out_shape=[jax.ShapeDtypeStruct((rows, LANES), F32),
                   jax.ShapeDtypeStruct((rows, LANES), F32)],
        scratch_shapes=[pltpu.VMEM((1, LANES), F32)],
        compiler_params=_params(("parallel", "arbitrary")),
        name="log_forget",
    )(xb, w_f, b_f)


HALO = 8
SCAN_ROWS = 8


def _lru_gates(xc, wa, ba, wx, bx, lam):
    xcb = xc.astype(BF16)
    r = jax.nn.sigmoid(jnp.dot(xcb, wa, preferred_element_type=F32) + ba)
    ig = jax.nn.sigmoid(jnp.dot(xcb, wx, preferred_element_type=F32) + bx)
    log_a = -LRU_C * r * _softplus(-lam)
    a = jnp.exp(log_a)
    u = jnp.sqrt(-_expm1(2.0 * log_a)) * (ig * xc)
    return a, u


def _lru_prompt_kernel(xa_ref, cw_ref, cb_ref, wa_ref, ba_ref, wx_ref, bx_ref, lam_ref,
                       o_ref, ext_ref, h_ref, a_ref, u_ref):
    i = pl.program_id(1)
    tt = xa_ref.shape[0]

    @pl.when(i == 0)
    def _():
        ext_ref[0:HALO, :] = jnp.zeros((HALO, ext_ref.shape[1]), F32)
        h_ref[...] = jnp.zeros_like(h_ref)

    @pl.when(i > 0)
    def _():
        ext_ref[0:HALO, :] = ext_ref[tt:tt + HALO, :]

    ext_ref[HALO:HALO + tt, :] = xa_ref[...]
    cw = cw_ref[...]
    xc = cb_ref[...]
    for j in range(CONV_W):
        off = HALO - (CONV_W - 1) + j
        xc = xc + cw[j:j + 1, :] * ext_ref[off:off + tt, :]
    a, u = _lru_gates(xc, wa_ref[...], ba_ref[...], wx_ref[...], bx_ref[...], lam_ref[...])
    a_ref[...] = a
    u_ref[...] = u

    row = lax.broadcasted_iota(jnp.int32, (SCAN_ROWS, a.shape[1]), 0)

    def body(g, h_prev):
        r0 = pl.multiple_of(g * SCAN_ROWS, SCAN_ROWS)
        ag = a_ref[pl.ds(r0, SCAN_ROWS), :]
        ug = u_ref[pl.ds(r0, SCAN_ROWS), :]
        s = 1
        while s < SCAN_ROWS:
            ug = ug + ag * jnp.where(row >= s, pltpu.roll(ug, s, axis=0), 0.0)
            ag = ag * jnp.where(row >= s, pltpu.roll(ag, s, axis=0), 1.0)
            s *= 2
        hg = ag * h_prev + ug
        o_ref[pl.ds(r0, SCAN_ROWS), :] = hg
        return hg[SCAN_ROWS - 1:, :]

    h_ref[...] = lax.fori_loop(0, tt // SCAN_ROWS, body, h_ref[...])


def _lru_prompt(proj, col_blk, n_seq, seq_len, tt, cw, cb, wa, ba, wx, bx, lam):
    c = cw.shape[1]
    nt = seq_len // tt
    const = lambda b, i: (0, 0)
    return pl.pallas_call(
        _lru_prompt_kernel,
        grid=(n_seq, nt),
        in_specs=[pl.BlockSpec((tt, c), lambda b, i: (b * nt + i, col_blk)),
                  pl.BlockSpec((CONV_W, c), const), pl.BlockSpec((1, c), const),
                  pl.BlockSpec((c, c), const), pl.BlockSpec((1, c), const),
                  pl.BlockSpec((c, c), const), pl.BlockSpec((1, c), const),
                  pl.BlockSpec((1, c), const)],
        out_specs=pl.BlockSpec((tt, c), lambda b, i: (b * nt + i, 0)),
        out_shape=jax.ShapeDtypeStruct((n_seq * seq_len, c), F32),
        scratch_shapes=[pltpu.VMEM((HALO + tt, c), F32), pltpu.VMEM((1, c), F32),
                        pltpu.VMEM((tt, c), F32), pltpu.VMEM((tt, c), F32)],
        compiler_params=_params(("parallel", "arbitrary")),
        name="lru_prompt",
    )(proj, cw, cb, wa, ba, wx, bx, lam)


POOL_HALO = 16


def _pool_prompt_kernel(x_ref, w_ref, sc_ref, o_ref, ext_ref):
    i = pl.program_id(1)
    tt = x_ref.shape[0]

    @pl.when(i == 0)
    def _():
        ext_ref[0:POOL_HALO, :] = jnp.zeros((POOL_HALO, ext_ref.shape[1]), F32)

    @pl.when(i > 0)
    def _():
        ext_ref[0:POOL_HALO, :] = ext_ref[tt:tt + POOL_HALO, :]

    ext_ref[POOL_HALO:POOL_HALO + tt, :] = x_ref[...]
    pos = i * tt + lax.broadcasted_iota(jnp.int32, (tt, GROUP_W), 0)
    for g, win in enumerate(POOL_WINDOWS):
        lo, hi = g * GROUP_W, (g + 1) * GROUP_W
        wsum = ext_ref[POOL_HALO:POOL_HALO + tt, lo:hi]
        for j in range(1, win):
            wsum = wsum + ext_ref[POOL_HALO - j:POOL_HALO - j + tt, lo:hi]
        cnt = jnp.minimum(win, pos + 1).astype(F32)
        pooled = wsum / cnt - x_ref[:, lo:hi]
        y = jnp.dot(pooled.astype(BF16), w_ref[g], preferred_element_type=F32)
        o_ref[:, lo:hi] = y * sc_ref[:, lo:hi]


def _pool_prompt(proj, col_blk, n_seq, seq_len, tt, w, scale):
    c = scale.shape[1]
    nt = seq_len // tt
    return pl.pallas_call(
        _pool_prompt_kernel,
        grid=(n_seq, nt),
        in_specs=[pl.BlockSpec((tt, c), lambda b, i: (b * nt + i, col_blk)),
                  pl.BlockSpec(w.shape, lambda b, i: (0, 0, 0)),
                  pl.BlockSpec((1, c), lambda b, i: (0, 0))],
        out_specs=pl.BlockSpec((tt, c), lambda b, i: (b * nt + i, 0)),
        out_shape=jax.ShapeDtypeStruct((n_seq * seq_len, c), F32),
        scratch_shapes=[pltpu.VMEM((POOL_HALO + tt, c), F32)],
        compiler_params=_params(("parallel", "arbitrary")),
        name="pool_prompt",
    )(proj, w, scale)


def _sgu_prompt_kernel(du_ref, dv_ref, g_ref, b_ref, w_ref, sbt_ref, o_ref):
    tt = du_ref.shape[0]
    vn = _layer_norm(_gelu(dv_ref[...]), g_ref[...], b_ref[...])
    vnb = vn.astype(BF16)
    gu = _gelu(du_ref[...])
    row = lax.broadcasted_iota(jnp.int32, (SGU_CHUNK, SGU_CHUNK), 0)
    col = lax.broadcasted_iota(jnp.int32, (SGU_CHUNK, SGU_CHUNK), 1)
    sbt = sbt_ref[...]
    for g in range(N_BRANCH):
        lo, hi = g * GROUP_W, (g + 1) * GROUP_W
        wt = jnp.where(col <= row, w_ref[g], 0.0).astype(BF16)
        bcol = sbt[:, g:g + 1]
        for ch in range(tt // SGU_CHUNK):
            r0, r1 = ch * SGU_CHUNK, (ch + 1) * SGU_CHUNK
            mixed = jnp.dot(wt, vnb[r0:r1, lo:hi], preferred_element_type=F32) + bcol
            o_ref[r0:r1, lo:hi] = gu[r0:r1, lo:hi] * mixed


def _sgu_prompt(proj, col_u, col_v, n_rows, tt, ln_g, ln_b, w, sbt):
    c = ln_g.shape[1]
    return pl.pallas_call(
        _sgu_prompt_kernel,
        grid=(n_rows // tt,),
        in_specs=[pl.BlockSpec((tt, c), lambda i: (i, col_u)),
                  pl.BlockSpec((tt, c), lambda i: (i, col_v)),
                  pl.BlockSpec((1, c), lambda i: (0, 0)),
                  pl.BlockSpec((1, c), lambda i: (0, 0)),
                  pl.BlockSpec(w.shape, lambda i: (0, 0, 0)),
                  pl.BlockSpec(sbt.shape, lambda i: (0, 0))],
        out_specs=pl.BlockSpec((tt, c), lambda i: (i, 0)),
        out_shape=jax.ShapeDtypeStruct((n_rows, c), F32),
        compiler_params=_params(("parallel",)),
        name="sgu_prompt",
    )(proj, proj, ln_g, ln_b, w, sbt)


def _fox_prompt_kernel(q_ref, k_ref, v_ref, cq_ref, ck_ref, o_ref, kb_ref, vb_ref,
                       m_ref, l_ref, acc_ref, *, scale, tk):
    p = pl.program_id(1)
    i = pl.program_id(2)
    tq = q_ref.shape[0]

    @pl.when(i == 0)
    def _():
        kb_ref[...] = k_ref[...].astype(BF16)
        vb_ref[...] = v_ref[...].astype(BF16)

    lane = lax.broadcasted_iota(jnp.int32, (tq, LANES), 1)
    q = q_ref[...] * scale
    cq_all = cq_ref[...]
    qh, cq = [], []
    for hh in range(2):
        head_lanes = (lane >= hh * HEAD_DIM) & (lane < (hh + 1) * HEAD_DIM)
        qh.append(jnp.where(head_lanes, q, 0.0).astype(BF16))
        cq.append(jnp.sum(jnp.where(lane == 2 * p + hh, cq_all, 0.0), axis=1, keepdims=True))
    m_ref[...] = jnp.full(m_ref.shape, NEG_INF, F32)
    l_ref[...] = jnp.zeros_like(l_ref)
    acc_ref[...] = jnp.zeros_like(acc_ref)

    def chunk(j, on_diagonal):
        k0 = pl.multiple_of(j * tk, tk)
        kk = kb_ref[pl.ds(k0, tk), :]
        vv = vb_ref[pl.ds(k0, tk), :]
        for hh in range(2):
            s = _dot_nt(qh[hh], kk)
            ck = ck_ref[0, pl.ds(2 * p + hh, 1), pl.ds(k0, tk)]
            s = s + (cq[hh] - ck)
            if on_diagonal:
                qrow = lax.broadcasted_iota(jnp.int32, (tq, tk), 0)
                kcol = lax.broadcasted_iota(jnp.int32, (tq, tk), 1)
                s = jnp.where(kcol <= qrow, s, NEG_INF)
            m_old = m_ref[hh]
            m_new = jnp.maximum(m_old, jnp.max(s, axis=1, keepdims=True))
            alpha = jnp.exp(m_old - m_new)
            pr = jnp.concatenate([jnp.exp(s[:, c0:c0 + LANES] - m_new)
                                  for c0 in range(0, tk, LANES)], axis=1)
            l_ref[hh] = alpha * l_ref[hh] + jnp.sum(pr, axis=1, keepdims=True)
            acc_ref[hh] = alpha * acc_ref[hh] + jnp.dot(pr.astype(BF16), vv,
                                                        preferred_element_type=F32)
            m_ref[hh] = m_new

    def body(j, carry):
        chunk(j, False)
        return carry

    lax.fori_loop(0, i, body, 0)
    chunk(i, True)
    o_ref[...] = jnp.where(lane < HEAD_DIM, acc_ref[0] / l_ref[0], acc_ref[1] / l_ref[1])


def _fox_prompt(proj, c_rows, c_t, n_seq, seq_len, tq, tk, col_q, col_k, col_v):
    assert tq == tk
    nq = seq_len // tq
    n_pairs = N_HEADS // 2
    kern = functools.partial(_fox_prompt_kernel, scale=HEAD_DIM ** -0.5, tk=tk)
    return pl.pallas_call(
        kern,
        grid=(n_seq, n_pairs, nq),
        in_specs=[pl.BlockSpec((tq, LANES), lambda b, p, i: (b * nq + i, col_q + p)),
                  pl.BlockSpec((seq_len, LANES), lambda b, p, i: (b, col_k + p)),
                  pl.BlockSpec((seq_len, LANES), lambda b, p, i: (b, col_v + p)),
                  pl.BlockSpec((tq, LANES), lambda b, p, i: (b * nq + i, 0)),
                  pl.BlockSpec((1, N_HEADS, seq_len), lambda b, p, i: (b, 0, 0))],
        out_specs=pl.BlockSpec((tq, LANES), lambda b, p, i: (b * nq + i, p)),
        out_shape=jax.ShapeDtypeStruct((n_seq * seq_len, n_pairs * LANES), F32),
        scratch_shapes=[pltpu.VMEM((seq_len, LANES), BF16), pltpu.VMEM((seq_len, LANES), BF16),
                        pltpu.VMEM((2, tq, LANES), F32), pltpu.VMEM((2, tq, LANES), F32),
                        pltpu.VMEM((2, tq, LANES), F32)],
        compiler_params=_params(("parallel", "parallel", "arbitrary")),
        name="fox_prompt",
    )(proj, proj, proj, c_rows, c_t)


def _lane_prefix(x):
    n = x.shape[1]
    lane = lax.broadcasted_iota(jnp.int32, x.shape, 1)
    s = 1
    while s < n:
        x = x + jnp.where(lane >= s, pltpu.roll(x, s, axis=1), 0.0)
        s *= 2
    return x


def _page_prefix_kernel(lf_ref, p_ref, tot_ref):
    lf = lf_ref[...]
    p_ref[...] = _lane_prefix(lf)
    tot_ref[...] = jnp.broadcast_to(jnp.sum(lf, axis=1, keepdims=True), lf.shape)


def _page_prefix(lf_rows, tr):
    rows, page = lf_rows.shape
    spec = pl.BlockSpec((tr, page), lambda i: (i, 0))
    shp = jax.ShapeDtypeStruct((rows, page), F32)
    return pl.pallas_call(
        _page_prefix_kernel,
        grid=(rows // tr,),
        in_specs=[spec], out_specs=[spec, spec], out_shape=[shp, shp],
        compiler_params=_params(("parallel",)),
        name="page_prefix",
    )(lf_rows)


def _fox_sample_kernel(pt_ref, q_ref, *refs, scale, n_new, n_grp):
    k_refs = refs[0:n_grp]
    v_refs = refs[n_grp:2 * n_grp]
    p_refs = refs[2 * n_grp:3 * n_grp]
    t_refs = refs[3 * n_grp:4 * n_grp]
    kn_ref, vn_ref, lfn_ref, o_ref, qb_ref, m_ref, l_ref, acc_ref, cb_ref = refs[4 * n_grp:]
    j = pl.program_id(1)
    n_steps = pl.num_programs(1)
    n_tok, c = q_ref.shape[1], q_ref.shape[2]
    rows = n_tok * N_HEADS
    row_i = lax.broadcasted_iota(jnp.int32, (N_HEADS, c), 0)
    lane_i = lax.broadcasted_iota(jnp.int32, (N_HEADS, c), 1)
    own_lanes = (lane_i >= row_i * HEAD_DIM) & (lane_i < (row_i + 1) * HEAD_DIM)

    @pl.when(j == 0)
    def _():
        q = q_ref[0] * scale
        for t in range(n_tok):
            qt = jnp.broadcast_to(q[t:t + 1, :], (N_HEADS, c))
            qb_ref[t * N_HEADS:(t + 1) * N_HEADS, :] = jnp.where(own_lanes, qt, 0.0).astype(BF16)
        m_ref[...] = jnp.full(m_ref.shape, NEG_INF, F32)
        l_ref[...] = jnp.zeros_like(l_ref)
        acc_ref[...] = jnp.zeros_like(acc_ref)
        cb_ref[...] = jnp.zeros_like(cb_ref)

    qb = qb_ref[...]

    def update(s_list, v_list):
        m_old = m_ref[...]
        m_new = m_old
        for s in s_list:
            m_new = jnp.maximum(m_new, jnp.max(s, axis=1, keepdims=True))
        alpha = jnp.exp(m_old - m_new)
        l_new = alpha * l_ref[...]
        acc = alpha * acc_ref[...]
        for s, vv in zip(s_list, v_list):
            pr = jnp.exp(s - m_new)
            l_new = l_new + jnp.sum(pr, axis=1, keepdims=True)
            acc = acc + _dot_nt(pr.astype(BF16), vv.astype(BF16))
        l_ref[...] = l_new
        acc_ref[...] = acc
        m_ref[...] = m_new

    base = cb_ref[...]
    s_list, v_list = [], []
    for g in range(n_grp):
        c_k = base + p_refs[g][0, 0]
        base = base + t_refs[g][0, 0]
        s = jnp.dot(qb, k_refs[g][0, 0].astype(BF16), preferred_element_type=F32)
        s_list.append(s - jnp.concatenate([c_k] * n_tok, axis=0))
        v_list.append(v_refs[g][0, 0])
    cb_ref[...] = base
    update(s_list, v_list)

    @pl.when(j == n_steps - 1)
    def _():
        nk_new = kn_ref.shape[2]
        c_new = cb_ref[:, 0:nk_new] + _lane_prefix(lfn_ref[0])
        s = jnp.dot(qb, kn_ref[0].astype(BF16), preferred_element_type=F32)
        s = s - jnp.concatenate([c_new] * n_tok, axis=0)
        r_n = lax.broadcasted_iota(jnp.int32, (rows, nk_new), 0)
        c_n = lax.broadcasted_iota(jnp.int32, (rows, nk_new), 1)
        ok = (c_n * N_HEADS <= (r_n | (N_HEADS - 1))) & (c_n < n_new)
        update([jnp.where(ok, s, NEG_INF)], [vn_ref[0]])
        out = acc_ref[...] / l_ref[...]
        for t in range(n_tok):
            blk = out[t * N_HEADS:(t + 1) * N_HEADS, :]
            o_ref[0, t:t + 1, :] = jnp.sum(jnp.where(own_lanes, blk, 0.0), axis=0, keepdims=True)


def _fox_sample(page_table, layer, q, k_t, v_t, pfx, tot, k_new, v_new, lf_new, n_new, n_grp):
    db, dt, c = q.shape
    n_pages = page_table.shape[1]
    page = k_t.shape[3]
    nkn = k_new.shape[2]
    rows = dt * N_HEADS
    kern = functools.partial(_fox_sample_kernel, scale=HEAD_DIM ** -0.5, n_new=n_new, n_grp=n_grp)

    def paged(shape2, g):
        return pl.BlockSpec((1, 1) + shape2,
                            lambda b, j, pt: (layer, pt[b, j * n_grp + g], 0, 0))

    per_b = lambda shape2: pl.BlockSpec((1,) + shape2, lambda b, j, pt: (b, 0, 0))
    grid_spec = pltpu.PrefetchScalarGridSpec(
        num_scalar_prefetch=1,
        grid=(db, n_pages // n_grp),
        in_specs=[per_b((dt, c))]
        + [paged((c, page), g) for g in range(n_grp)]
        + [paged((c, page), g) for g in range(n_grp)]
        + [paged((N_HEADS, page), g) for g in range(n_grp)]
        + [paged((N_HEADS, page), g) for g in range(n_grp)]
        + [per_b((c, nkn)), per_b((c, nkn)), per_b((N_HEADS, nkn))],
        out_specs=per_b((dt, c)),
        scratch_shapes=[pltpu.VMEM((rows, c), BF16), pltpu.VMEM((rows, 1), F32),
                        pltpu.VMEM((rows, 1), F32), pltpu.VMEM((rows, c), F32),
                        pltpu.VMEM((N_HEADS, page), F32)],
    )
    return pl.pallas_call(
        kern,
        grid_spec=grid_spec,
        out_shape=jax.ShapeDtypeStruct((db, dt, c), F32),
        compiler_params=_params(("parallel", "arbitrary")),
        name="fox_sample",
    )(page_table, q, *([k_t] * n_grp), *([v_t] * n_grp), *([pfx] * n_grp), *([tot] * n_grp),
      k_new, v_new, lf_new)


def _sample_mix_kernel(xa_ref, sconv_ref, h0_ref, xc_ref, spool_ref, du_ref, dv_ref,
                       cw_ref, cb_ref, wa_ref, ba_ref, wx_ref, bx_ref, lam_ref,
                       pw_ref, psc_ref, lng_ref, lnb_ref, sw_ref, sb_ref,
                       oa_ref, oc_ref, vn_ref, od_ref, *, pos0):
    nt = xa_ref.shape[0]
    cw = cw_ref[...]
    ext = [sconv_ref[j] for j in range(CONV_W - 1)] + [xa_ref[t] for t in range(nt)]
    h = h0_ref[...]
    for t in range(nt):
        xc = cb_ref[...]
        for j in range(CONV_W):
            xc = xc + cw[j:j + 1, :] * ext[t + j]
        a, u = _lru_gates(xc, wa_ref[...], ba_ref[...], wx_ref[...], bx_ref[...], lam_ref[...])
        h = a * h + u
        oa_ref[t] = h
    pext = [spool_ref[j] for j in range(POOL_HIST)] + [xc_ref[t] for t in range(nt)]
    for t in range(nt):
        x_t = pext[POOL_HIST + t]
        for g, win in enumerate(POOL_WINDOWS):
            lo, hi = g * GROUP_W, (g + 1) * GROUP_W
            wsum = x_t[:, lo:hi]
            for j in range(1, win):
                wsum = wsum + pext[POOL_HIST + t - j][:, lo:hi]
            cnt = float(min(win, pos0 + t + 1))
            pooled = wsum / cnt - x_t[:, lo:hi]
            y = jnp.dot(pooled.astype(BF16), pw_ref[g], preferred_element_type=F32)
            oc_ref[t, :, lo:hi] = y * psc_ref[:, lo:hi]
    vns = []
    for t in range(nt):
        vn = _layer_norm(_gelu(dv_ref[t]), lng_ref[...], lnb_ref[...])
        vn_ref[t] = vn
        vns.append(vn)
    for t in range(nt):
        mixed = sb_ref[t]
        for s in range(t + 1):
            mixed = mixed + sw_ref[t, s] * vns[s]
        od_ref[t] = _gelu(du_ref[t]) * mixed


def _sample_mix(xa, sconv, h0, xc, spool, du, dv, cw, cb, wa, ba, wx, bx, lam,
                pw, psc, lng, lnb, sw, sb, pos0):
    shp = jax.ShapeDtypeStruct(xa.shape, F32)
    kern = functools.partial(_sample_mix_kernel, pos0=pos0)
    return pl.pallas_call(
        kern,
        out_shape=[shp, shp, shp, shp],
        compiler_params=pltpu.CompilerParams(vmem_limit_bytes=VMEM_LIMIT_V7X),
        name="sample_mix",
    )(xa, sconv, h0, xc, spool, du, dv, cw, cb, wa, ba, wx, bx, lam, pw, psc, lng, lnb, sw, sb)


def _merge_kernel(x_ref, oa_ref, ob_ref, oc_ref, od_ref, g0_ref, g1_ref, g2_ref, g3_ref,
                  wb_ref, o_ref):
    x = x_ref[...]
    acc = None
    for n, (o_n, g_n) in enumerate(((oa_ref, g0_ref), (ob_ref, g1_ref),
                                    (oc_ref, g2_ref), (od_ref, g3_ref))):
        gate = jax.nn.sigmoid(_dot_nt(x, g_n[0]))
        br = jnp.dot(o_n[...].astype(BF16), wb_ref[0, n], preferred_element_type=F32)
        acc = gate * br if acc is None else acc + gate * br
    o_ref[...] = acc.astype(o_ref.dtype)


def _merge(xb, oa, ob, oc, od, wg_all, gate_col0, wb_all, layer, tm, tn):
    m, d = xb.shape
    c = oa.shape[1]
    ncol = d // tn
    blk0 = gate_col0 // tn
    o_spec = pl.BlockSpec((tm, c), lambda i, j: (i, 0))
    g_specs = [pl.BlockSpec((1, tn, d), (lambda n: (lambda i, j: (layer, blk0 + n * ncol + j, 0)))(n))
               for n in range(N_BRANCH)]
    return pl.pallas_call(
        _merge_kernel,
        grid=(m // tm, ncol),
        in_specs=[pl.BlockSpec((tm, d), lambda i, j: (i, 0)), o_spec, o_spec, o_spec, o_spec]
        + g_specs + [pl.BlockSpec((1, N_BRANCH, c, tn), lambda i, j: (layer, 0, 0, j))],
        out_specs=pl.BlockSpec((tm, tn), lambda i, j: (i, j)),
        out_shape=jax.ShapeDtypeStruct((m, d), BF16),
        compiler_params=_params(("parallel", "arbitrary")),
        name="gated_merge",
    )(xb, oa, ob, oc, od, wg_all, wg_all, wg_all, wg_all, wb_all)


def _out_ln_kernel(m_ref, w_ref, x_ref, g_ref, b_ref, o_ref, ob_ref, *, alpha):
    y = jnp.dot(m_ref[...], w_ref[0], preferred_element_type=F32)
    out = _layer_norm(alpha * x_ref[...] + y, g_ref[...], b_ref[...])
    o_ref[...] = out
    ob_ref[...] = out.astype(BF16)


def _out_ln(merged, w_all, layer, x, g, b, alpha, tm):
    m, d = x.shape
    row = pl.BlockSpec((tm, d), lambda i: (i, 0))
    vec = pl.BlockSpec((1, d), lambda i: (0, 0))
    return pl.pallas_call(
        functools.partial(_out_ln_kernel, alpha=alpha),
        grid=(m // tm,),
        in_specs=[row, pl.BlockSpec((1, d, d), lambda i: (layer, 0, 0)), row, vec, vec],
        out_specs=[row, row],
        out_shape=[jax.ShapeDtypeStruct((m, d), F32), jax.ShapeDtypeStruct((m, d), BF16)],
        compiler_params=_params(("parallel",)),
        name="out_proj_ln",
    )(merged, w_all, x, g, b)


def _add_ln_kernel(y_ref, x_ref, g_ref, b_ref, o_ref, ob_ref, *, alpha):
    out = _layer_norm(alpha * x_ref[...] + y_ref[...], g_ref[...], b_ref[...])
    o_ref[...] = out
    ob_ref[...] = out.astype(BF16)


def _add_ln(y, x, g, b, alpha, tm):
    m, d = x.shape
    row = pl.BlockSpec((tm, d), lambda i: (i, 0))
    vec = pl.BlockSpec((1, d), lambda i: (0, 0))
    return pl.pallas_call(
        functools.partial(_add_ln_kernel, alpha=alpha),
        grid=(m // tm,),
        in_specs=[row, row, vec, vec],
        out_specs=[row, row],
        out_shape=[jax.ShapeDtypeStruct((m, d), F32), jax.ShapeDtypeStruct((m, d), BF16)],
        compiler_params=_params(("parallel",)),
        name="peer_add_ln",
    )(y, x, g, b)


LOG2E = 1.4426950408889634


def _top_rows(s, k):
    vals = []
    for _ in range(k):
        mx = jnp.max(s, axis=0, keepdims=True)
        vals.append(mx)
        s = jnp.where(s == mx, NEG_INF, s)
    return vals


def _peer_score_kernel(x_ref, wq_ref, k1_ref, k2_ref, s1_ref, s2_ref, st_ref):
    half = k1_ref.shape[1]
    tm = x_ref.shape[0]
    q = jnp.dot(x_ref[...], wq_ref[0], preferred_element_type=F32)
    s1 = _dot_nt(k1_ref[...], q[:, :half].astype(BF16)) * LOG2E
    s2 = _dot_nt(k2_ref[...], q[:, half:].astype(BF16)) * LOG2E
    for c0 in range(0, tm, LANES):
        a = s1[:, c0:c0 + LANES]
        b = s2[:, c0:c0 + LANES]
        v1 = _top_rows(a, PEER_TOPK + 1)
        v2 = _top_rows(b, PEER_TOPK + 1)
        m1, m2 = v1[0], v2[0]
        v2cat = jnp.concatenate([v - m2 for v in v2[:PEER_TOPK]], axis=0)
        cand = jnp.concatenate([(v - m1) + v2cat for v in v1[:PEER_TOPK]], axis=0)
        top = _top_rows(cand, PEER_TOPK + 1)
        z = jnp.exp2(top[0])
        for kk in range(1, PEER_TOPK):
            z = z + jnp.exp2(top[kk])
        lz = jnp.log(z) * LOG2E
        next_sum = jnp.maximum(top[PEER_TOPK],
                               jnp.maximum(v1[PEER_TOPK] - m1, v2[PEER_TOPK] - m2))
        tau = 0.5 * (top[PEER_TOPK - 1] + next_sum) - lz
        s1_ref[0, :, c0:c0 + LANES] = (a - m1) - lz
        s2_ref[0, :, c0:c0 + LANES] = b - m2
        st_ref[0, :, c0:c0 + LANES] = jnp.broadcast_to(tau, (8, LANES))


def _peer_scores(xb, wq_all, layer, k1, k2, tm):
    n, d = xb.shape
    nk, half = k1.shape
    big = pl.BlockSpec((1, nk, tm), lambda i, h: (h, 0, i))
    return pl.pallas_call(
        _peer_score_kernel,
        grid=(n // tm, N_HEADS),
        in_specs=[pl.BlockSpec((tm, d), lambda i, h: (i, 0)),
                  pl.BlockSpec((1, d, 2 * half), lambda i, h: (layer, 0, h)),
                  pl.BlockSpec(k1.shape, lambda i, h: (0, 0)),
                  pl.BlockSpec(k2.shape, lambda i, h: (0, 0))],
        out_specs=[big, big, pl.BlockSpec((1, 8, tm), lambda i, h: (h, 0, i))],
        out_shape=[jax.ShapeDtypeStruct((N_HEADS, nk, n), F32),
                   jax.ShapeDtypeStruct((N_HEADS, nk, n), F32),
                   jax.ShapeDtypeStruct((N_HEADS, 8, n), F32)],
        compiler_params=_params(("parallel", "arbitrary")),
        name="peer_scores",
    )(xb, wq_all, k1, k2)


PEER_LAG = 2
PEER_TE = 512


def _peer_dense_kernel(x_ref, u_ref, vt_ref, s1_ref, s2_ref, st_ref, y_ref,
                       acc_ref, h_new, h_old, g_new, g_old):
    j = pl.program_id(1)
    n_steps = pl.num_programs(1)
    n_e = n_steps - PEER_LAG
    te = u_ref.shape[1]
    nk = s2_ref.shape[1]

    @pl.when(j == 0)
    def _():
        acc_ref[...] = jnp.zeros_like(acc_ref)
        h_old[...] = jnp.zeros_like(h_old)
        g_old[...] = jnp.zeros_like(g_old)

    acc_ref[...] += jnp.dot(vt_ref[0, 0], g_old[...], preferred_element_type=F32)

    tile = jnp.clip(j - 1, 0, n_e - 1)
    tm = x_ref.shape[0]
    for r in range(te // nk):
        i1 = tile * (te // nk) + r
        rs = slice(r * nk, (r + 1) * nk)
        s1_rows = [s1_ref[h, pl.ds(i1, 1), :] for h in range(N_HEADS)]
        for c0 in range(0, tm, LANES):
            ls = slice(c0, c0 + LANES)
            w = None
            for h in range(N_HEADS):
                arg = s1_rows[h][:, ls] + s2_ref[h, :, ls]
                term = jnp.where(arg >= st_ref[h, 0:1, ls], jnp.exp2(arg), 0.0)
                w = term if w is None else w + term
            g_new[rs, ls] = (_gelu(h_old[rs, ls]) * w).astype(BF16)

    h_new[...] = _dot_nt(u_ref[0], x_ref[...])

    g_old[...] = g_new[...]
    h_old[...] = h_new[...]

    @pl.when(j == n_steps - 1)
    def _():
        y_ref[...] = acc_ref[...].T


def _peer_dense(xb, u_all, vt_all, layer, s1t, s2t, st, tm):
    n_rows, d = xb.shape
    n_e, te = vt_all.shape[1], vt_all.shape[3]
    nk = s1t.shape[1]
    tok3 = lambda i, j: (0, 0, i)
    return pl.pallas_call(
        _peer_dense_kernel,
        grid=(n_rows // tm, n_e + PEER_LAG),
        in_specs=[pl.BlockSpec((tm, d), lambda i, j: (i, 0)),
                  pl.BlockSpec((1, te, d), lambda i, j: (layer, jnp.minimum(j, n_e - 1), 0)),
                  pl.BlockSpec((1, 1, d, te),
                               lambda i, j: (layer, jnp.clip(j - PEER_LAG, 0, n_e - 1), 0, 0)),
                  pl.BlockSpec((N_HEADS, nk, tm), tok3),
                  pl.BlockSpec((N_HEADS, nk, tm), tok3),
                  pl.BlockSpec((N_HEADS, 8, tm), tok3)],
        out_specs=pl.BlockSpec((tm, d), lambda i, j: (i, 0)),
        out_shape=jax.ShapeDtypeStruct((n_rows, d), F32),
        scratch_shapes=[pltpu.VMEM((d, tm), F32),
                        pltpu.VMEM((te, tm), F32), pltpu.VMEM((te, tm), F32),
                        pltpu.VMEM((te, tm), BF16), pltpu.VMEM((te, tm), BF16)],
        compiler_params=_params(("parallel", "arbitrary")),
        name="peer_dense",
    )(xb, u_all, vt_all, s1t, s2t, st)


def _largest_divisor(n, cap, mult):
    best = None
    for t in range(mult, cap + 1, mult):
        if n % t == 0:
            best = t
    assert best is not None, (n, cap, mult)
    return best


def _block_diag(w):
    h, a, b = w.shape
    eye = jnp.eye(h, dtype=w.dtype)
    return (eye[:, None, :, None] * w[:, :, None, :]).reshape(h * a, h * b)


def _channel_mix(l, depth, x, xb, mixed, w, tm_merge, tm):
    alpha = (2 * depth) ** 0.25
    n = x.shape[0]
    merged = _merge(xb, *mixed, w["w_tail"], w["gate_col0"], w["wb"], l, tm_merge, 256)
    x1, x1b = _out_ln(merged, w["wo"], l, x, w["ln1_g"][l], w["ln1_b"][l], alpha,
                      _largest_divisor(n, 256, 16))
    s1t, s2t, st = _peer_scores(x1b, w["wq"], l, w["k1"][l], w["k2"][l], tm)
    y = _peer_dense(x1b, w["u"], w["vt"], l, s1t, s2t, st, tm)
    return _add_ln(y, x1, w["ln2_g"][l], w["ln2_b"][l], alpha, tm)


def kernel(x_prompt, x_sample, cache_k, cache_v, cache_logf, state_conv, state_lru, state_pool, page_table, w_in, conv_w, conv_b, lru_wa, lru_ba, lru_wx, lru_bx, lru_lambda, fox_bf, pool_w, pool_scale, sgu_ln_g, sgu_ln_b, sgu_w, sgu_b, w_branch, w_out, ln1_g, ln1_b, peer_wq, peer_k1, peer_k2, peer_u, peer_v, ln2_g, ln2_b):
    n_seq, seq_len, d = x_prompt.shape
    db, dt, _ = x_sample.shape
    depth = w_in.shape[0]
    c = d // N_BRANCH
    n_p, n_s = n_seq * seq_len, db * dt
    n_pool, page = cache_k.shape[1], cache_k.shape[2]
    n_pages = page_table.shape[1]
    past_len = n_pages * page
    assert c == N_HEADS * HEAD_DIM == N_BRANCH * GROUP_W
    assert seq_len % SGU_CHUNK == 0 and dt <= SGU_CHUNK and dt <= page and page == LANES

    fcol = 4 * c
    rows = lambda v: v.reshape(depth, 1, -1)
    w_in_t = w_in.transpose(0, 2, 1)
    w = dict(
        w_tail=w_in_t[:, fcol + N_HEADS:, :].astype(BF16),
        gate_col0=3 * c,
        wb=w_branch.astype(BF16), wo=w_out.astype(BF16), wq=peer_wq.astype(BF16),
        k1=peer_k1.astype(BF16), k2=peer_k2.astype(BF16),
        u=peer_u.astype(BF16),
        vt=peer_v.reshape(depth, -1, PEER_TE, d).transpose(0, 1, 3, 2).astype(BF16),
        ln1_g=rows(ln1_g), ln1_b=rows(ln1_b), ln2_g=rows(ln2_g), ln2_b=rows(ln2_b))
    w_head = w_in_t.astype(BF16)
    w_f = jnp.pad(w_in_t[:, fcol:fcol + N_HEADS, :], ((0, 0), (0, LANES - N_HEADS), (0, 0))).astype(BF16)
    b_f = jnp.pad(fox_bf, ((0, 0), (0, LANES - N_HEADS))).reshape(depth, 1, LANES)
    wa = jax.vmap(_block_diag)(lru_wa).astype(BF16)
    wx = jax.vmap(_block_diag)(lru_wx).astype(BF16)
    cb, ba, bx, lam = rows(conv_b), rows(lru_ba), rows(lru_bx), rows(lru_lambda)
    pw = pool_w.astype(BF16)
    psc, lng, lnb = rows(pool_scale), rows(sgu_ln_g), rows(sgu_ln_b)

    k_t = cache_k.transpose(0, 1, 3, 4, 2).reshape(depth, n_pool, c, page)
    v_t = cache_v.transpose(0, 1, 3, 4, 2).reshape(depth, n_pool, c, page)
    lf_rows = cache_logf.transpose(0, 1, 3, 2).reshape(depth * n_pool * N_HEADS, page)
    pfx, tot = _page_prefix(lf_rows, _largest_divisor(lf_rows.shape[0], 2048, 8))
    pfx = pfx.reshape(depth, n_pool, N_HEADS, page)
    tot = tot.reshape(depth, n_pool, N_HEADS, page)

    tm_p = _largest_divisor(n_p, 1024, 16)
    tt = _largest_divisor(seq_len, 512, 8)
    n_grp = _largest_divisor(n_pages, 16, 1)
    hd = (N_HEADS, HEAD_DIM)
    tmaj = lambda a: a.reshape(db, dt, -1).transpose(1, 0, 2)
    bmaj = lambda a: a.transpose(1, 0, 2).reshape(n_s, -1)
    pad_lanes = lambda a: jnp.pad(a, ((0, 0), (0, 0), (0, LANES - a.shape[2])))

    xp, xs = x_prompt.reshape(n_p, d), x_sample.reshape(n_s, d)
    xpb, xsb = xp.astype(BF16), xs.astype(BF16)
    p_states, s_states = [], []
    for l in range(depth):
        ph = _matmul(xpb, w_head, l, fcol, tm_p, 512)
        pt = _matmul(xpb, w["w_tail"], l, 3 * c, tm_p, 512)
        sh = _matmul(xsb, w_head, l, fcol, n_s, 512)
        st = _matmul(xsb, w["w_tail"], l, 3 * c, n_s, 512)
        lf_p, c_p = _logf(xpb, w_f[l], b_f[l], n_seq, seq_len, tt)
        lf_s, _ = _logf(xsb, w_f[l], b_f[l], 1, n_s, n_s)
        c_t = c_p[:, :N_HEADS].reshape(n_seq, seq_len, N_HEADS).transpose(0, 2, 1)

        oa_p = _lru_prompt(ph, 0, n_seq, seq_len, tt, conv_w[l], cb[l], wa[l], ba[l], wx[l], bx[l],
                           lam[l])
        oc_p = _pool_prompt(pt, 0, n_seq, seq_len, tt, pw[l], psc[l])
        od_p = _sgu_prompt(pt, 1, 2, n_p, tt, lng[l], lnb[l], sgu_w[l], sgu_b[l].T)
        ob_p = _fox_prompt(ph, c_p, c_t, n_seq, seq_len, tt, tt,
                           c // LANES, 2 * c // LANES, 3 * c // LANES)

        sw = jnp.repeat(sgu_w[l, :, :dt, :dt].transpose(1, 2, 0), GROUP_W, axis=2).reshape(dt, dt, 1, c)
        sb = jnp.repeat(sgu_b[l, :, :dt].T, GROUP_W, axis=1).reshape(dt, 1, c)
        oa_s, oc_s, vn_s, od_s = _sample_mix(
            tmaj(sh[:, :c]), state_conv[l].transpose(1, 0, 2), state_lru[l],
            tmaj(st[:, :c]), state_pool[l].transpose(1, 0, 2),
            tmaj(st[:, c:2 * c]), tmaj(st[:, 2 * c:]),
            conv_w[l], cb[l], wa[l], ba[l], wx[l], bx[l], lam[l], pw[l], psc[l], lng[l], lnb[l],
            sw, sb, past_len)
        new_t = lambda a: pad_lanes(a.reshape(db, dt, -1).transpose(0, 2, 1))
        ob_s = _fox_sample(page_table, l, sh[:, c:2 * c].reshape(db, dt, c), k_t, v_t, pfx, tot,
                           new_t(sh[:, 2 * c:3 * c]), new_t(sh[:, 3 * c:]),
                           new_t(lf_s[:, :N_HEADS]), dt, n_grp).reshape(n_s, c)

        p3 = lambda a: a.reshape(n_seq, seq_len, -1)
        p_states.append((
            ph[:, 2 * c:3 * c].reshape(n_seq, seq_len, *hd),
            ph[:, 3 * c:].reshape(n_seq, seq_len, *hd),
            lf_p[:, :N_HEADS].reshape(n_seq, seq_len, N_HEADS),
            p3(ph)[:, seq_len - (CONV_W - 1):, :c],
            p3(oa_p)[:, -1],
            p3(pt)[:, seq_len - POOL_HIST:, :c]))
        s3 = lambda a: a.reshape(db, dt, -1)
        s_states.append((
            sh[:, 2 * c:3 * c].reshape(db, dt, *hd),
            sh[:, 3 * c:].reshape(db, dt, *hd),
            lf_s[:, :N_HEADS].reshape(db, dt, N_HEADS),
            jnp.concatenate([state_conv[l], s3(sh)[:, :, :c]], axis=1)[:, -(CONV_W - 1):],
            oa_s[dt - 1],
            jnp.concatenate([state_pool[l], s3(st)[:, :, :c]], axis=1)[:, -POOL_HIST:],
            vn_s.transpose(1, 0, 2)))

        xp, xpb = _channel_mix(l, depth, xp, xpb, (oa_p, ob_p, oc_p, od_p), w, tm_p, 512)
        xs, xsb = _channel_mix(l, depth, xs, xsb,
                               (bmaj(oa_s), ob_s, bmaj(oc_s), bmaj(od_s)), w, n_s, n_s)

    yp = xp.reshape(n_seq, seq_len, d)
    ys = xs.reshape(db, dt, d)
    p_out = [jnp.stack([s[j] for s in p_states]) for j in range(6)]
    s_out = [jnp.stack([s[j] for s in s_states]) for j in range(7)]
    return (yp, ys, *p_out, *s_out)
```

```python
import functools

import jax
import jax.numpy as jnp
from jax import lax
from jax.experimental import pallas as pl
from jax.experimental.pallas import tpu as pltpu

F32 = jnp.float32
BF16 = jnp.bfloat16

N_BRANCH = 4
N_HEADS = 8
HEAD_DIM = 64
CONV_W = 4
LRU_C = 8.0
POOL_WINDOWS = (2, 4, 8, 16)
POOL_HIST = 15
GROUP_W = 128
SGU_CHUNK = 128
PEER_NKEYS = 128
PEER_TOPK = 16
LN_EPS = 1e-5
NEG_INF = float("-inf")

VMEM_LIMIT_V7X = 56 * 1024 * 1024
LANES = 128


def _params(semantics):
    return pltpu.CompilerParams(dimension_semantics=semantics, vmem_limit_bytes=VMEM_LIMIT_V7X)


def _gelu(x):
    return 0.5 * x * (1.0 + lax.erf(x * 0.7071067811865476))


def _softplus(z):
    return jnp.maximum(z, 0.0) + jnp.log1p(jnp.exp(-jnp.abs(z)))


def _log_sigmoid(z):
    return -_softplus(-z)


def _expm1(y):
    e = jnp.exp(y)
    return jnp.where(e == 1.0, y, (e - 1.0) * y / jnp.log(e))


def _layer_norm(x, g, b):
    mu = jnp.mean(x, axis=-1, keepdims=True)
    d = x - mu
    var = jnp.mean(d * d, axis=-1, keepdims=True)
    return d * lax.rsqrt(var + LN_EPS) * g + b


def _cumsum_rows(x):
    n = x.shape[0]
    rows = lax.broadcasted_iota(jnp.int32, x.shape, 0)
    s = 1
    while s < n:
        x = x + jnp.where(rows >= s, pltpu.roll(x, s, axis=0), 0.0)
        s *= 2
    return x


def _dot_nt(a, b):
    return lax.dot_general(a, b, (((1,), (1,)), ((), ())), preferred_element_type=F32)


def _matmul_kernel(x_ref, w_ref, o_ref):
    o_ref[...] = _dot_nt(x_ref[...], w_ref[0].astype(BF16))


def _matmul(xb, wt_all, layer, n_cols, tm, tn):
    m, k = xb.shape
    return pl.pallas_call(
        _matmul_kernel,
        grid=(m // tm, n_cols // tn),
        in_specs=[pl.BlockSpec((tm, k), lambda i, j: (i, 0)),
                  pl.BlockSpec((1, tn, k), lambda i, j: (layer, j, 0))],
        out_specs=pl.BlockSpec((tm, tn), lambda i, j: (i, j)),
        out_shape=jax.ShapeDtypeStruct((m, n_cols), F32),
        compiler_params=_params(("parallel", "arbitrary")),
        name="in_proj",
    )(xb, wt_all)


def _logf_kernel(x_ref, w_ref, b_ref, lf_ref, c_ref, carry_ref):
    i = pl.program_id(1)
    fl = _dot_nt(x_ref[...], w_ref[...]) + b_ref[...]
    lf = _log_sigmoid(fl)
    lf_ref[...] = lf

    @pl.when(i == 0)
    def _():
        carry_ref[...] = jnp.zeros_like(carry_ref)

    c = _cumsum_rows(lf) + carry_ref[...]
    c_ref[...] = c
    carry_ref[...] = c[c.shape[0] - 1:, :]


def _logf(xb, w_f, b_f, n_seq, seq_len, tt):
    k = xb.shape[1]
    nt = seq_len // tt
    rows = n_seq * seq_len
    return pl.pallas_call(
        _logf_kernel,
        grid=(n_seq, nt),
        in_specs=[pl.BlockSpec((tt, k), lambda b, i: (b * nt + i, 0)),
                  pl.BlockSpec((LANES, k), lambda b, i: (0, 0)),
                  pl.BlockSpec((1, LANES), lambda b, i: (0, 0))],
        out_specs=[pl.BlockSpec((tt, LANES), lambda b, i: (b * nt + i, 0)),
                   pl.BlockSpec((tt, LANES), lambda b, i: (b * nt + i, 0))],
        out_shape=[jax.ShapeDtypeStruct((rows, LANES), F32),
                   jax.ShapeDtypeStruct((rows, LANES), F32)],
        scratch_shapes=[pltpu.VMEM((1, LANES), F32)],
        compiler_params=_params(("parallel", "arbitrary")),
        name="log_forget",
    )(xb, w_f, b_f)


HALO = 8
SCAN_ROWS = 8


def _lru_gates(xc, wa, ba, wx, bx, lam):
    xcb = xc.astype(BF16)
    r = jax.nn.sigmoid(jnp.dot(xcb, wa, preferred_element_type=F32) + ba)
    ig = jax.nn.sigmoid(jnp.dot(xcb, wx, preferred_element_type=F32) + bx)
    log_a = -LRU_C * r * _softplus(-lam)
    a = jnp.exp(log_a)
    u = jnp.sqrt(-_expm1(2.0 * log_a)) * (ig * xc)
    return a, u


def _lru_prompt_kernel(xa_ref, cw_ref, cb_ref, wa_ref, ba_ref, wx_ref, bx_ref, lam_ref,
                       o_ref, ext_ref, h_ref, a_ref, u_ref):
    i = pl.program_id(1)
    tt = xa_ref.shape[0]

    @pl.when(i == 0)
    def _():
        ext_ref[0:HALO, :] = jnp.zeros((HALO, ext_ref.shape[1]), F32)
        h_ref[...] = jnp.zeros_like(h_ref)

    @pl.when(i > 0)
    def _():
        ext_ref[0:HALO, :] = ext_ref[tt:tt + HALO, :]

    ext_ref[HALO:HALO + tt, :] = xa_ref[...]
    cw = cw_ref[...]
    xc = cb_ref[...]
    for j in range(CONV_W):
        off = HALO - (CONV_W - 1) + j
        xc = xc + cw[j:j + 1, :] * ext_ref[off:off + tt, :]
    a, u = _lru_gates(xc, wa_ref[...], ba_ref[...], wx_ref[...], bx_ref[...], lam_ref[...])
    a_ref[...] = a
    u_ref[...] = u

    row = lax.broadcasted_iota(jnp.int32, (SCAN_ROWS, a.shape[1]), 0)

    def body(g, h_prev):
        r0 = pl.multiple_of(g * SCAN_ROWS, SCAN_ROWS)
        ag = a_ref[pl.ds(r0, SCAN_ROWS), :]
        ug = u_ref[pl.ds(r0, SCAN_ROWS), :]
        s = 1
        while s < SCAN_ROWS:
            ug = ug + ag * jnp.where(row >= s, pltpu.roll(ug, s, axis=0), 0.0)
            ag = ag * jnp.where(row >= s, pltpu.roll(ag, s, axis=0), 1.0)
            s *= 2
        hg = ag * h_prev + ug
        o_ref[pl.ds(r0, SCAN_ROWS), :] = hg
        return hg[SCAN_ROWS - 1:, :]

    h_ref[...] = lax.fori_loop(0, tt // SCAN_ROWS, body, h_ref[...])


def _lru_prompt(proj, col_blk, n_seq, seq_len, tt, cw, cb, wa, ba, wx, bx, lam):
    c = cw.shape[1]
    nt = seq_len // tt
    const = lambda b, i: (0, 0)
    return pl.pallas_call(
        _lru_prompt_kernel,
        grid=(n_seq, nt),
        in_specs=[pl.BlockSpec((tt, c), lambda b, i: (b * nt + i, col_blk)),
                  pl.BlockSpec((CONV_W, c), const), pl.BlockSpec((1, c), const),
                  pl.BlockSpec((c, c), const), pl.BlockSpec((1, c), const),
                  pl.BlockSpec((c, c), const), pl.BlockSpec((1, c), const),
                  pl.BlockSpec((1, c), const)],
        out_specs=pl.BlockSpec((tt, c), lambda b, i: (b * nt + i, 0)),
        out_shape=jax.ShapeDtypeStruct((n_seq * seq_len, c), F32),
        scratch_shapes=[pltpu.VMEM((HALO + tt, c), F32), pltpu.VMEM((1, c), F32),
                        pltpu.VMEM((tt, c), F32), pltpu.VMEM((tt, c), F32)],
        compiler_params=_params(("parallel", "arbitrary")),
        name="lru_prompt",
    )(proj, cw, cb, wa, ba, wx, bx, lam)


POOL_HALO = 16


def _pool_prompt_kernel(x_ref, w_ref, sc_ref, o_ref, ext_ref):
    i = pl.program_id(1)
    tt = x_ref.shape[0]

    @pl.when(i == 0)
    def _():
        ext_ref[0:POOL_HALO, :] = jnp.zeros((POOL_HALO, ext_ref.shape[1]), F32)

    @pl.when(i > 0)
    def _():
        ext_ref[0:POOL_HALO, :] = ext_ref[tt:tt + POOL_HALO, :]

    ext_ref[POOL_HALO:POOL_HALO + tt, :] = x_ref[...]
    pos = i * tt + lax.broadcasted_iota(jnp.int32, (tt, GROUP_W), 0)
    for g, win in enumerate(POOL_WINDOWS):
        lo, hi = g * GROUP_W, (g + 1) * GROUP_W
        wsum = ext_ref[POOL_HALO:POOL_HALO + tt, lo:hi]
        for j in range(1, win):
            wsum = wsum + ext_ref[POOL_HALO - j:POOL_HALO - j + tt, lo:hi]
        cnt = jnp.minimum(win, pos + 1).astype(F32)
        pooled = wsum / cnt - x_ref[:, lo:hi]
        y = jnp.dot(pooled.astype(BF16), w_ref[g], preferred_element_type=F32)
        o_ref[:, lo:hi] = y * sc_ref[:, lo:hi]


def _pool_prompt(proj, col_blk, n_seq, seq_len, tt, w, scale):
    c = scale.shape[1]
    nt = seq_len // tt
    return pl.pallas_call(
        _pool_prompt_kernel,
        grid=(n_seq, nt),
        in_specs=[pl.BlockSpec((tt, c), lambda b, i: (b * nt + i, col_blk)),
                  pl.BlockSpec(w.shape, lambda b, i: (0, 0, 0)),
                  pl.BlockSpec((1, c), lambda b, i: (0, 0))],
        out_specs=pl.BlockSpec((tt, c), lambda b, i: (b * nt + i, 0)),
        out_shape=jax.ShapeDtypeStruct((n_seq * seq_len, c), F32),
        scratch_shapes=[pltpu.VMEM((POOL_HALO + tt, c), F32)],
        compiler_params=_params(("parallel", "arbitrary")),
        name="pool_prompt",
    )(proj, w, scale)


def _sgu_prompt_kernel(du_ref, dv_ref, g_ref, b_ref, w_ref, sbt_ref, o_ref):
    tt = du_ref.shape[0]
    vn = _layer_norm(_gelu(dv_ref[...]), g_ref[...], b_ref[...])
    vnb = vn.astype(BF16)
    gu = _gelu(du_ref[...])
    row = lax.broadcasted_iota(jnp.int32, (SGU_CHUNK, SGU_CHUNK), 0)
    col = lax.broadcasted_iota(jnp.int32, (SGU_CHUNK, SGU_CHUNK), 1)
    sbt = sbt_ref[...]
    for g in range(N_BRANCH):
        lo, hi = g * GROUP_W, (g + 1) * GROUP_W
        wt = jnp.where(col <= row, w_ref[g], 0.0).astype(BF16)
        bcol = sbt[:, g:g + 1]
        for ch in range(tt // SGU_CHUNK):
            r0, r1 = ch * SGU_CHUNK, (ch + 1) * SGU_CHUNK
            mixed = jnp.dot(wt, vnb[r0:r1, lo:hi], preferred_element_type=F32) + bcol
            o_ref[r0:r1, lo:hi] = gu[r0:r1, lo:hi] * mixed


def _sgu_prompt(proj, col_u, col_v, n_rows, tt, ln_g, ln_b, w, sbt):
    c = ln_g.shape[1]
    return pl.pallas_call(
        _sgu_prompt_kernel,
        grid=(n_rows // tt,),
        in_specs=[pl.BlockSpec((tt, c), lambda i: (i, col_u)),
                  pl.BlockSpec((tt, c), lambda i: (i, col_v)),
                  pl.BlockSpec((1, c), lambda i: (0, 0)),
                  pl.BlockSpec((1, c), lambda i: (0, 0)),
                  pl.BlockSpec(w.shape, lambda i: (0, 0, 0)),
                  pl.BlockSpec(sbt.shape, lambda i: (0, 0))],
        out_specs=pl.BlockSpec((tt, c), lambda i: (i, 0)),
        out_shape=jax.ShapeDtypeStruct((n_rows, c), F32),
        compiler_params=_params(("parallel",)),
        name="sgu_prompt",
    )(proj, proj, ln_g, ln_b, w, sbt)


def _fox_prompt_kernel(q_ref, k_ref, v_ref, cq_ref, ck_ref, o_ref, kb_ref, vb_ref,
                       m_ref, l_ref, acc_ref, *, scale, tk):
    p = pl.program_id(1)
    i = pl.program_id(2)
    tq = q_ref.shape[0]

    @pl.when(i == 0)
    def _():
        kb_ref[...] = k_ref[...].astype(BF16)
        vb_ref[...] = v_ref[...].astype(BF16)

    lane = lax.broadcasted_iota(jnp.int32, (tq, LANES), 1)
    q = q_ref[...] * scale
    cq_all = cq_ref[...]
    qh, cq = [], []
    for hh in range(2):
        head_lanes = (lane >= hh * HEAD_DIM) & (lane < (hh + 1) * HEAD_DIM)
        qh.append(jnp.where(head_lanes, q, 0.0).astype(BF16))
        cq.append(jnp.sum(jnp.where(lane == 2 * p + hh, cq_all, 0.0), axis=1, keepdims=True))
    m_ref[...] = jnp.full(m_ref.shape, NEG_INF, F32)
    l_ref[...] = jnp.zeros_like(l_ref)
    acc_ref[...] = jnp.zeros_like(acc_ref)

    def chunk(j, on_diagonal):
        k0 = pl.multiple_of(j * tk, tk)
        kk = kb_ref[pl.ds(k0, tk), :]
        vv = vb_ref[pl.ds(k0, tk), :]
        for hh in range(2):
            s = _dot_nt(qh[hh], kk)
            ck = ck_ref[0, pl.ds(2 * p + hh, 1), pl.ds(k0, tk)]
            s = s + (cq[hh] - ck)
            if on_diagonal:
                qrow = lax.broadcasted_iota(jnp.int32, (tq, tk), 0)
                kcol = lax.broadcasted_iota(jnp.int32, (tq, tk), 1)
                s = jnp.where(kcol <= qrow, s, NEG_INF)
            m_old = m_ref[hh]
            m_new = jnp.maximum(m_old, jnp.max(s, axis=1, keepdims=True))
            alpha = jnp.exp(m_old - m_new)
            pr = jnp.concatenate([jnp.exp(s[:, c0:c0 + LANES] - m_new)
                                  for c0 in range(0, tk, LANES)], axis=1)
            l_ref[hh] = alpha * l_ref[hh] + jnp.sum(pr, axis=1, keepdims=True)
            acc_ref[hh] = alpha * acc_ref[hh] + jnp.dot(pr.astype(BF16), vv,
                                                        preferred_element_type=F32)
            m_ref[hh] = m_new

    def body(j, carry):
        chunk(j, False)
        return carry

    lax.fori_loop(0, i, body, 0)
    chunk(i, True)
    o_ref[...] = jnp.where(lane < HEAD_DIM, acc_ref[0] / l_ref[0], acc_ref[1] / l_ref[1])


def _fox_prompt(proj, c_rows, c_t, n_seq, seq_len, tq, tk, col_q, col_k, col_v):
    assert tq == tk
    nq = seq_len // tq
    n_pairs = N_HEADS // 2
    kern = functools.partial(_fox_prompt_kernel, scale=HEAD_DIM ** -0.5, tk=tk)
    return pl.pallas_call(
        kern,
        grid=(n_seq, n_pairs, nq),
        in_specs=[pl.BlockSpec((tq, LANES), lambda b, p, i: (b * nq + i, col_q + p)),
                  pl.BlockSpec((seq_len, LANES), lambda b, p, i: (b, col_k + p)),
                  pl.BlockSpec((seq_len, LANES), lambda b, p, i: (b, col_v + p)),
                  pl.BlockSpec((tq, LANES), lambda b, p, i: (b * nq + i, 0)),
                  pl.BlockSpec((1, N_HEADS, seq_len), lambda b, p, i: (b, 0, 0))],
        out_specs=pl.BlockSpec((tq, LANES), lambda b, p, i: (b * nq + i, p)),
        out_shape=jax.ShapeDtypeStruct((n_seq * seq_len, n_pairs * LANES), F32),
        scratch_shapes=[pltpu.VMEM((seq_len, LANES), BF16), pltpu.VMEM((seq_len, LANES), BF16),
                        pltpu.VMEM((2, tq, LANES), F32), pltpu.VMEM((2, tq, LANES), F32),
                        pltpu.VMEM((2, tq, LANES), F32)],
        compiler_params=_params(("parallel", "parallel", "arbitrary")),
        name="fox_prompt",
    )(proj, proj, proj, c_rows, c_t)


def _lane_prefix(x):
    n = x.shape[1]
    lane = lax.broadcasted_iota(jnp.int32, x.shape, 1)
    s = 1
    while s < n:
        x = x + jnp.where(lane >= s, pltpu.roll(x, s, axis=1), 0.0)
        s *= 2
    return x


def _page_prefix_kernel(lf_ref, p_ref, tot_ref):
    lf = lf_ref[...]
    p_ref[...] = _lane_prefix(lf)
    tot_ref[...] = jnp.broadcast_to(jnp.sum(lf, axis=1, keepdims=True), lf.shape)


def _page_prefix(lf_rows, tr):
    rows, page = lf_rows.shape
    spec = pl.BlockSpec((tr, page), lambda i: (i, 0))
    shp = jax.ShapeDtypeStruct((rows, page), F32)
    return pl.pallas_call(
        _page_prefix_kernel,
        grid=(rows // tr,),
        in_specs=[spec], out_specs=[spec, spec], out_shape=[shp, shp],
        compiler_params=_params(("parallel",)),
        name="page_prefix",
    )(lf_rows)


def _fox_sample_kernel(pt_ref, q_ref, *refs, scale, n_new, n_grp):
    k_refs = refs[0:n_grp]
    v_refs = refs[n_grp:2 * n_grp]
    p_refs = refs[2 * n_grp:3 * n_grp]
    t_refs = refs[3 * n_grp:4 * n_grp]
    kn_ref, vn_ref, lfn_ref, o_ref, qb_ref, m_ref, l_ref, acc_ref, cb_ref = refs[4 * n_grp:]
    j = pl.program_id(1)
    n_steps = pl.num_programs(1)
    n_tok, c = q_ref.shape[1], q_ref.shape[2]
    rows = n_tok * N_HEADS
    row_i = lax.broadcasted_iota(jnp.int32, (N_HEADS, c), 0)
    lane_i = lax.broadcasted_iota(jnp.int32, (N_HEADS, c), 1)
    own_lanes = (lane_i >= row_i * HEAD_DIM) & (lane_i < (row_i + 1) * HEAD_DIM)

    @pl.when(j == 0)
    def _():
        q = q_ref[0] * scale
        for t in range(n_tok):
            qt = jnp.broadcast_to(q[t:t + 1, :], (N_HEADS, c))
            qb_ref[t * N_HEADS:(t + 1) * N_HEADS, :] = jnp.where(own_lanes, qt, 0.0).astype(BF16)
        m_ref[...] = jnp.full(m_ref.shape, NEG_INF, F32)
        l_ref[...] = jnp.zeros_like(l_ref)
        acc_ref[...] = jnp.zeros_like(acc_ref)
        cb_ref[...] = jnp.zeros_like(cb_ref)

    qb = qb_ref[...]

    def update(s_list, v_list):
        m_old = m_ref[...]
        m_new = m_old
        for s in s_list:
            m_new = jnp.maximum(m_new, jnp.max(s, axis=1, keepdims=True))
        alpha = jnp.exp(m_old - m_new)
        l_new = alpha * l_ref[...]
        acc = alpha * acc_ref[...]
        for s, vv in zip(s_list, v_list):
            pr = jnp.exp(s - m_new)
            l_new = l_new + jnp.sum(pr, axis=1, keepdims=True)
            acc = acc + _dot_nt(pr.astype(BF16), vv.astype(BF16))
        l_ref[...] = l_new
        acc_ref[...] = acc
        m_ref[...] = m_new

    base = cb_ref[...]
    s_list, v_list = [], []
    for g in range(n_grp):
        c_k = base + p_refs[g][0, 0]
        base = base + t_refs[g][0, 0]
        s = jnp.dot(qb, k_refs[g][0, 0].astype(BF16), preferred_element_type=F32)
        s_list.append(s - jnp.concatenate([c_k] * n_tok, axis=0))
        v_list.append(v_refs[g][0, 0])
    cb_ref[...] = base
    update(s_list, v_list)

    @pl.when(j == n_steps - 1)
    def _():
        nk_new = kn_ref.shape[2]
        c_new = cb_ref[:, 0:nk_new] + _lane_prefix(lfn_ref[0])
        s = jnp.dot(qb, kn_ref[0].astype(BF16), preferred_element_type=F32)
        s = s - jnp.concatenate([c_new] * n_tok, axis=0)
        r_n = lax.broadcasted_iota(jnp.int32, (rows, nk_new), 0)
        c_n = lax.broadcasted_iota(jnp.int32, (rows, nk_new), 1)
        ok = (c_n * N_HEADS <= (r_n | (N_HEADS - 1))) & (c_n < n_new)
        update([jnp.where(ok, s, NEG_INF)], [vn_ref[0]])
        out = acc_ref[...] / l_ref[...]
        for t in range(n_tok):
            blk = out[t * N_HEADS:(t + 1) * N_HEADS, :]
            o_ref[0, t:t + 1, :] = jnp.sum(jnp.where(own_lanes, blk, 0.0), axis=0, keepdims=True)


def _fox_sample(page_table, layer, q, k_t, v_t, pfx, tot, k_new, v_new, lf_new, n_new, n_grp):
    db, dt, c = q.shape
    n_pages = page_table.shape[1]
    page = k_t.shape[3]
    nkn = k_new.shape[2]
    rows = dt * N_HEADS
    kern = functools.partial(_fox_sample_kernel, scale=HEAD_DIM ** -0.5, n_new=n_new, n_grp=n_grp)

    def paged(shape2, g):
        return pl.BlockSpec((1, 1) + shape2,
                            lambda b, j, pt: (layer, pt[b, j * n_grp + g], 0, 0))

    per_b = lambda shape2: pl.BlockSpec((1,) + shape2, lambda b, j, pt: (b, 0, 0))
    grid_spec = pltpu.PrefetchScalarGridSpec(
        num_scalar_prefetch=1,
        grid=(db, n_pages // n_grp),
        in_specs=[per_b((dt, c))]
        + [paged((c, page), g) for g in range(n_grp)]
        + [paged((c, page), g) for g in range(n_grp)]
        + [paged((N_HEADS, page), g) for g in range(n_grp)]
        + [paged((N_HEADS, page), g) for g in range(n_grp)]
        + [per_b((c, nkn)), per_b((c, nkn)), per_b((N_HEADS, nkn))],
        out_specs=per_b((dt, c)),
        scratch_shapes=[pltpu.VMEM((rows, c), BF16), pltpu.VMEM((rows, 1), F32),
                        pltpu.VMEM((rows, 1), F32), pltpu.VMEM((rows, c), F32),
                        pltpu.VMEM((N_HEADS, page), F32)],
    )
    return pl.pallas_call(
        kern,
        grid_spec=grid_spec,
        out_shape=jax.ShapeDtypeStruct((db, dt, c), F32),
        compiler_params=_params(("parallel", "arbitrary")),
        name="fox_sample",
    )(page_table, q, *([k_t] * n_grp), *([v_t] * n_grp), *([pfx] * n_grp), *([tot] * n_grp),
      k_new, v_new, lf_new)


def _sample_mix_kernel(xa_ref, sconv_ref, h0_ref, xc_ref, spool_ref, du_ref, dv_ref,
                       cw_ref, cb_ref, wa_ref, ba_ref, wx_ref, bx_ref, lam_ref,
                       pw_ref, psc_ref, lng_ref, lnb_ref, sw_ref, sb_ref,
                       oa_ref, oc_ref, vn_ref, od_ref, *, pos0):
    nt = xa_ref.shape[0]
    cw = cw_ref[...]
    ext = [sconv_ref[j] for j in range(CONV_W - 1)] + [xa_ref[t] for t in range(nt)]
    h = h0_ref[...]
    for t in range(nt):
        xc = cb_ref[...]
        for j in range(CONV_W):
            xc = xc + cw[j:j + 1, :] * ext[t + j]
        a, u = _lru_gates(xc, wa_ref[...], ba_ref[...], wx_ref[...], bx_ref[...], lam_ref[...])
        h = a * h + u
        oa_ref[t] = h
    pext = [spool_ref[j] for j in range(POOL_HIST)] + [xc_ref[t] for t in range(nt)]
    for t in range(nt):
        x_t = pext[POOL_HIST + t]
        for g, win in enumerate(POOL_WINDOWS):
            lo, hi = g * GROUP_W, (g + 1) * GROUP_W
            wsum = x_t[:, lo:hi]
            for j in range(1, win):
                wsum = wsum + pext[POOL_HIST + t - j][:, lo:hi]
            cnt = float(min(win, pos0 + t + 1))
            pooled = wsum / cnt - x_t[:, lo:hi]
            y = jnp.dot(pooled.astype(BF16), pw_ref[g], preferred_element_type=F32)
            oc_ref[t, :, lo:hi] = y * psc_ref[:, lo:hi]
    vns = []
    for t in range(nt):
        vn = _layer_norm(_gelu(dv_ref[t]), lng_ref[...], lnb_ref[...])
        vn_ref[t] = vn
        vns.append(vn)
    for t in range(nt):
        mixed = sb_ref[t]
        for s in range(t + 1):
            mixed = mixed + sw_ref[t, s] * vns[s]
        od_ref[t] = _gelu(du_ref[t]) * mixed


def _sample_mix(xa, sconv, h0, xc, spool, du, dv, cw, cb, wa, ba, wx, bx, lam,
                pw, psc, lng, lnb, sw, sb, pos0):
    shp = jax.ShapeDtypeStruct(xa.shape, F32)
    kern = functools.partial(_sample_mix_kernel, pos0=pos0)
    return pl.pallas_call(
        kern,
        out_shape=[shp, shp, shp, shp],
        compiler_params=pltpu.CompilerParams(vmem_limit_bytes=VMEM_LIMIT_V7X),
        name="sample_mix",
    )(xa, sconv, h0, xc, spool, du, dv, cw, cb, wa, ba, wx, bx, lam, pw, psc, lng, lnb, sw, sb)


def _merge_kernel(x_ref, oa_ref, ob_ref, oc_ref, od_ref, g0_ref, g1_ref, g2_ref, g3_ref,
                  wb_ref, o_ref):
    x = x_ref[...]
    acc = None
    for n, (o_n, g_n) in enumerate(((oa_ref, g0_ref), (ob_ref, g1_ref),
                                    (oc_ref, g2_ref), (od_ref, g3_ref))):
        gate = jax.nn.sigmoid(_dot_nt(x, g_n[0]))
        br = jnp.dot(o_n[...].astype(BF16), wb_ref[0, n], preferred_element_type=F32)
        acc = gate * br if acc is None else acc + gate * br
    o_ref[...] = acc.astype(o_ref.dtype)


def _merge(xb, oa, ob, oc, od, wg_all, gate_col0, wb_all, layer, tm, tn):
    m, d = xb.shape
    c = oa.shape[1]
    ncol = d // tn
    blk0 = gate_col0 // tn
    o_spec = pl.BlockSpec((tm, c), lambda i, j: (i, 0))
    g_specs = [pl.BlockSpec((1, tn, d), (lambda n: (lambda i, j: (layer, blk0 + n * ncol + j, 0)))(n))
               for n in range(N_BRANCH)]
    return pl.pallas_call(
        _merge_kernel,
        grid=(m // tm, ncol),
        in_specs=[pl.BlockSpec((tm, d), lambda i, j: (i, 0)), o_spec, o_spec, o_spec, o_spec]
        + g_specs + [pl.BlockSpec((1, N_BRANCH, c, tn), lambda i, j: (layer, 0, 0, j))],
        out_specs=pl.BlockSpec((tm, tn), lambda i, j: (i, j)),
        out_shape=jax.ShapeDtypeStruct((m, d), BF16),
        compiler_params=_params(("parallel", "arbitrary")),
        name="gated_merge",
    )(xb, oa, ob, oc, od, wg_all, wg_all, wg_all, wg_all, wb_all)


def _out_ln_kernel(m_ref, w_ref, x_ref, g_ref, b_ref, o_ref, ob_ref, *, alpha):
    y = jnp.dot(m_ref[...], w_ref[0], preferred_element_type=F32)
    out = _layer_norm(alpha * x_ref[...] + y, g_ref[...], b_ref[...])
    o_ref[...] = out
    ob_ref[...] = out.astype(BF16)


def _out_ln(merged, w_all, layer, x, g, b, alpha, tm):
    m, d = x.shape
    row = pl.BlockSpec((tm, d), lambda i: (i, 0))
    vec = pl.BlockSpec((1, d), lambda i: (0, 0))
    return pl.pallas_call(
        functools.partial(_out_ln_kernel, alpha=alpha),
        grid=(m // tm,),
        in_specs=[row, pl.BlockSpec((1, d, d), lambda i: (layer, 0, 0)), row, vec, vec],
        out_specs=[row, row],
        out_shape=[jax.ShapeDtypeStruct((m, d), F32), jax.ShapeDtypeStruct((m, d), BF16)],
        compiler_params=_params(("parallel",)),
        name="out_proj_ln",
    )(merged, w_all, x, g, b)


def _add_ln_kernel(y_ref, x_ref, g_ref, b_ref, o_ref, ob_ref, *, alpha):
    out = _layer_norm(alpha * x_ref[...] + y_ref[...], g_ref[...], b_ref[...])
    o_ref[...] = out
    ob_ref[...] = out.astype(BF16)


def _add_ln(y, x, g, b, alpha, tm):
    m, d = x.shape
    row = pl.BlockSpec((tm, d), lambda i: (i, 0))
    vec = pl.BlockSpec((1, d), lambda i: (0, 0))
    return pl.pallas_call(
        functools.partial(_add_ln_kernel, alpha=alpha),
        grid=(m // tm,),
        in_specs=[row, row, vec, vec],
        out_specs=[row, row],
        out_shape=[jax.ShapeDtypeStruct((m, d), F32), jax.ShapeDtypeStruct((m, d), BF16)],
        compiler_params=_params(("parallel",)),
        name="peer_add_ln",
    )(y, x, g, b)


LOG2E = 1.4426950408889634


def _top_rows(s, k):
    vals = []
    for _ in range(k):
        mx = jnp.max(s, axis=0, keepdims=True)
        vals.append(mx)
        s = jnp.where(s == mx, NEG_INF, s)
    return vals


def _peer_score_kernel(x_ref, wq_ref, k1_ref, k2_ref, s1_ref, s2_ref, st_ref):
    half = k1_ref.shape[1]
    tm = x_ref.shape[0]
    q = jnp.dot(x_ref[...], wq_ref[0], preferred_element_type=F32)
    s1 = _dot_nt(k1_ref[...], q[:, :half].astype(BF16)) * LOG2E
    s2 = _dot_nt(k2_ref[...], q[:, half:].astype(BF16)) * LOG2E
    for c0 in range(0, tm, LANES):
        a = s1[:, c0:c0 + LANES]
        b = s2[:, c0:c0 + LANES]
        v1 = _top_rows(a, PEER_TOPK + 1)
        v2 = _top_rows(b, PEER_TOPK + 1)
        m1, m2 = v1[0], v2[0]
        v2cat = jnp.concatenate([v - m2 for v in v2[:PEER_TOPK]], axis=0)
        v2lo = v2cat[0:8]
        brow = lax.broadcasted_iota(jnp.int32, v2lo.shape, 0)
        parts = [(v1[0] - m1) + v2cat]
        for rank in range(1, PEER_TOPK):
            nb = (PEER_TOPK + 1) // (rank + 1)
            parts.append(jnp.where(brow < nb, (v1[rank] - m1) + v2lo, NEG_INF))
        cand = jnp.concatenate(parts, axis=0)
        top = _top_rows(cand, PEER_TOPK + 1)
        z = jnp.exp2(top[0])
        for kk in range(1, PEER_TOPK):
            z = z + jnp.exp2(top[kk])
        lz = jnp.log(z) * LOG2E
        next_sum = jnp.maximum(top[PEER_TOPK],
                               jnp.maximum(v1[PEER_TOPK] - m1, v2[PEER_TOPK] - m2))
        tau = 0.5 * (top[PEER_TOPK - 1] + next_sum) - lz
        s1_ref[0, :, c0:c0 + LANES] = (a - m1) - lz
        s2_ref[0, :, c0:c0 + LANES] = b - m2
        st_ref[0, :, c0:c0 + LANES] = jnp.broadcast_to(tau, (8, LANES))


def _peer_scores(xb, wq_all, layer, k1, k2, tm):
    n, d = xb.shape
    nk, half = k1.shape
    big = pl.BlockSpec((1, nk, tm), lambda i, h: (h, 0, i))
    return pl.pallas_call(
        _peer_score_kernel,
        grid=(n // tm, N_HEADS),
        in_specs=[pl.BlockSpec((tm, d), lambda i, h: (i, 0)),
                  pl.BlockSpec((1, d, 2 * half), lambda i, h: (layer, 0, h)),
                  pl.BlockSpec(k1.shape, lambda i, h: (0, 0)),
                  pl.BlockSpec(k2.shape, lambda i, h: (0, 0))],
        out_specs=[big, big, pl.BlockSpec((1, 8, tm), lambda i, h: (h, 0, i))],
        out_shape=[jax.ShapeDtypeStruct((N_HEADS, nk, n), F32),
                   jax.ShapeDtypeStruct((N_HEADS, nk, n), F32),
                   jax.ShapeDtypeStruct((N_HEADS, 8, n), F32)],
        compiler_params=_params(("parallel", "arbitrary")),
        name="peer_scores",
    )(xb, wq_all, k1, k2)


PEER_LAG = 2
PEER_TE = 512


def _peer_dense_kernel(x_ref, u_ref, vt_ref, s1_ref, s2_ref, st_ref, y_ref,
                       acc_ref, h_new, h_old, g_new, g_old):
    j = pl.program_id(1)
    n_steps = pl.num_programs(1)
    n_e = n_steps - PEER_LAG
    te = u_ref.shape[1]
    nk = s2_ref.shape[1]

    @pl.when(j == 0)
    def _():
        acc_ref[...] = jnp.zeros_like(acc_ref)
        h_old[...] = jnp.zeros_like(h_old)
        g_old[...] = jnp.zeros_like(g_old)

    acc_ref[...] += jnp.dot(vt_ref[0, 0], g_old[...], preferred_element_type=F32)

    tile = jnp.clip(j - 1, 0, n_e - 1)
    tm = x_ref.shape[0]
    for r in range(te // nk):
        i1 = tile * (te // nk) + r
        rs = slice(r * nk, (r + 1) * nk)
        s1_rows = [s1_ref[h, pl.ds(i1, 1), :] for h in range(N_HEADS)]
        for c0 in range(0, tm, LANES):
            ls = slice(c0, c0 + LANES)
            w = None
            for h in range(N_HEADS):
                arg = s1_rows[h][:, ls] + s2_ref[h, :, ls]
                term = jnp.where(arg >= st_ref[h, 0:1, ls], jnp.exp2(arg), 0.0)
                w = term if w is None else w + term
            g_new[rs, ls] = (_gelu(h_old[rs, ls]) * w).astype(BF16)

    h_new[...] = _dot_nt(u_ref[0], x_ref[...])

    g_old[...] = g_new[...]
    h_old[...] = h_new[...]

    @pl.when(j == n_steps - 1)
    def _():
        y_ref[...] = acc_ref[...].T


def _peer_dense(xb, u_all, vt_all, layer, s1t, s2t, st, tm):
    n_rows, d = xb.shape
    n_e, te = vt_all.shape[1], vt_all.shape[3]
    nk = s1t.shape[1]
    tok3 = lambda i, j: (0, 0, i)
    return pl.pallas_call(
        _peer_dense_kernel,
        grid=(n_rows // tm, n_e + PEER_LAG),
        in_specs=[pl.BlockSpec((tm, d), lambda i, j: (i, 0)),
                  pl.BlockSpec((1, te, d), lambda i, j: (layer, jnp.minimum(j, n_e - 1), 0)),
                  pl.BlockSpec((1, 1, d, te),
                               lambda i, j: (layer, jnp.clip(j - PEER_LAG, 0, n_e - 1), 0, 0)),
                  pl.BlockSpec((N_HEADS, nk, tm), tok3),
                  pl.BlockSpec((N_HEADS, nk, tm), tok3),
                  pl.BlockSpec((N_HEADS, 8, tm), tok3)],
        out_specs=pl.BlockSpec((tm, d), lambda i, j: (i, 0)),
        out_shape=jax.ShapeDtypeStruct((n_rows, d), F32),
        scratch_shapes=[pltpu.VMEM((d, tm), F32),
                        pltpu.VMEM((te, tm), F32), pltpu.VMEM((te, tm), F32),
                        pltpu.VMEM((te, tm), BF16), pltpu.VMEM((te, tm), BF16)],
        compiler_params=_params(("parallel", "arbitrary")),
        name="peer_dense",
    )(xb, u_all, vt_all, s1t, s2t, st)


def _largest_divisor(n, cap, mult):
    best = None
    for t in range(mult, cap + 1, mult):
        if n % t == 0:
            best = t
    assert best is not None, (n, cap, mult)
    return best


def _block_diag(w):
    h, a, b = w.shape
    eye = jnp.eye(h, dtype=w.dtype)
    return (eye[:, None, :, None] * w[:, :, None, :]).reshape(h * a, h * b)


def _channel_mix(l, depth, x, xb, mixed, w, tm_merge, tm):
    alpha = (2 * depth) ** 0.25
    n = x.shape[0]
    merged = _merge(xb, *mixed, w["w_tail"], w["gate_col0"], w["wb"], l, tm_merge, 256)
    x1, x1b = _out_ln(merged, w["wo"], l, x, w["ln1_g"][l], w["ln1_b"][l], alpha,
                      _largest_divisor(n, 256, 16))
    s1t, s2t, st = _peer_scores(x1b, w["wq"], l, w["k1"][l], w["k2"][l], tm)
    y = _peer_dense(x1b, w["u"], w["vt"], l, s1t, s2t, st, tm)
    return _add_ln(y, x1, w["ln2_g"][l], w["ln2_b"][l], alpha, tm)


def kernel(x_prompt, x_sample, cache_k, cache_v, cache_logf, state_conv, state_lru, state_pool, page_table, w_in, conv_w, conv_b, lru_wa, lru_ba, lru_wx, lru_bx, lru_lambda, fox_bf, pool_w, pool_scale, sgu_ln_g, sgu_ln_b, sgu_w, sgu_b, w_branch, w_out, ln1_g, ln1_b, peer_wq, peer_k1, peer_k2, peer_u, peer_v, ln2_g, ln2_b):
    n_seq, seq_len, d = x_prompt.shape
    db, dt, _ = x_sample.shape
    depth = w_in.shape[0]
    c = d // N_BRANCH
    n_p, n_s = n_seq * seq_len, db * dt
    n_pool, page = cache_k.shape[1], cache_k.shape[2]
    n_pages = page_table.shape[1]
    past_len = n_pages * page
    assert c == N_HEADS * HEAD_DIM == N_BRANCH * GROUP_W
    assert seq_len % SGU_CHUNK == 0 and dt <= SGU_CHUNK and dt <= page and page == LANES

    fcol = 4 * c
    rows = lambda v: v.reshape(depth, 1, -1)
    w_in_t = w_in.transpose(0, 2, 1)
    w = dict(
        w_tail=w_in_t[:, fcol + N_HEADS:, :].astype(BF16),
        gate_col0=3 * c,
        wb=w_branch.astype(BF16), wo=w_out.astype(BF16), wq=peer_wq.astype(BF16),
        k1=peer_k1.astype(BF16), k2=peer_k2.astype(BF16),
        u=peer_u.astype(BF16),
        vt=peer_v.reshape(depth, -1, PEER_TE, d).transpose(0, 1, 3, 2).astype(BF16),
        ln1_g=rows(ln1_g), ln1_b=rows(ln1_b), ln2_g=rows(ln2_g), ln2_b=rows(ln2_b))
    w_head = w_in_t.astype(BF16)
    w_f = jnp.pad(w_in_t[:, fcol:fcol + N_HEADS, :], ((0, 0), (0, LANES - N_HEADS), (0, 0))).astype(BF16)
    b_f = jnp.pad(fox_bf, ((0, 0), (0, LANES - N_HEADS))).reshape(depth, 1, LANES)
    wa = jax.vmap(_block_diag)(lru_wa).astype(BF16)
    wx = jax.vmap(_block_diag)(lru_wx).astype(BF16)
    cb, ba, bx, lam = rows(conv_b), rows(lru_ba), rows(lru_bx), rows(lru_lambda)
    pw = pool_w.astype(BF16)
    psc, lng, lnb = rows(pool_scale), rows(sgu_ln_g), rows(sgu_ln_b)

    k_t = cache_k.transpose(0, 1, 3, 4, 2).reshape(depth, n_pool, c, page)
    v_t = cache_v.transpose(0, 1, 3, 4, 2).reshape(depth, n_pool, c, page)
    lf_rows = cache_logf.transpose(0, 1, 3, 2).reshape(depth * n_pool * N_HEADS, page)
    pfx, tot = _page_prefix(lf_rows, _largest_divisor(lf_rows.shape[0], 2048, 8))
    pfx = pfx.reshape(depth, n_pool, N_HEADS, page)
    tot = tot.reshape(depth, n_pool, N_HEADS, page)

    tm_p = _largest_divisor(n_p, 1024, 16)
    tt = _largest_divisor(seq_len, 512, 8)
    n_grp = _largest_divisor(n_pages, 16, 1)
    hd = (N_HEADS, HEAD_DIM)
    tmaj = lambda a: a.reshape(db, dt, -1).transpose(1, 0, 2)
    bmaj = lambda a: a.transpose(1, 0, 2).reshape(n_s, -1)
    pad_lanes = lambda a: jnp.pad(a, ((0, 0), (0, 0), (0, LANES - a.shape[2])))

    xp, xs = x_prompt.reshape(n_p, d), x_sample.reshape(n_s, d)
    xpb, xsb = xp.astype(BF16), xs.astype(BF16)
    p_states, s_states = [], []
    for l in range(depth):
        ph = _matmul(xpb, w_head, l, fcol, tm_p, 512)
        pt = _matmul(xpb, w["w_tail"], l, 3 * c, tm_p, 512)
        sh = _matmul(xsb, w_head, l, fcol, n_s, 512)
        st = _matmul(xsb, w["w_tail"], l, 3 * c, n_s, 512)
        lf_p, c_p = _logf(xpb, w_f[l], b_f[l], n_seq, seq_len, tt)
        lf_s, _ = _logf(xsb, w_f[l], b_f[l], 1, n_s, n_s)
        c_t = c_p[:, :N_HEADS].reshape(n_seq, seq_len, N_HEADS).transpose(0, 2, 1)

        oa_p = _lru_prompt(ph, 0, n_seq, seq_len, tt, conv_w[l], cb[l], wa[l], ba[l], wx[l], bx[l],
                           lam[l])
        oc_p = _pool_prompt(pt, 0, n_seq, seq_len, tt, pw[l], psc[l])
        od_p = _sgu_prompt(pt, 1, 2, n_p, tt, lng[l], lnb[l], sgu_w[l], sgu_b[l].T)
        ob_p = _fox_prompt(ph, c_p, c_t, n_seq, seq_len, tt, tt,
                           c // LANES, 2 * c // LANES, 3 * c // LANES)

        sw = jnp.repeat(sgu_w[l, :, :dt, :dt].transpose(1, 2, 0), GROUP_W, axis=2).reshape(dt, dt, 1, c)
        sb = jnp.repeat(sgu_b[l, :, :dt].T, GROUP_W, axis=1).reshape(dt, 1, c)
        oa_s, oc_s, vn_s, od_s = _sample_mix(
            tmaj(sh[:, :c]), state_conv[l].transpose(1, 0, 2), state_lru[l],
            tmaj(st[:, :c]), state_pool[l].transpose(1, 0, 2),
            tmaj(st[:, c:2 * c]), tmaj(st[:, 2 * c:]),
            conv_w[l], cb[l], wa[l], ba[l], wx[l], bx[l], lam[l], pw[l], psc[l], lng[l], lnb[l],
            sw, sb, past_len)
        new_t = lambda a: pad_lanes(a.reshape(db, dt, -1).transpose(0, 2, 1))
        ob_s = _fox_sample(page_table, l, sh[:, c:2 * c].reshape(db, dt, c), k_t, v_t, pfx, tot,
                           new_t(sh[:, 2 * c:3 * c]), new_t(sh[:, 3 * c:]),
                           new_t(lf_s[:, :N_HEADS]), dt, n_grp).reshape(n_s, c)

        p3 = lambda a: a.reshape(n_seq, seq_len, -1)
        p_states.append((
            ph[:, 2 * c:3 * c].reshape(n_seq, seq_len, *hd),
            ph[:, 3 * c:].reshape(n_seq, seq_len, *hd),
            lf_p[:, :N_HEADS].reshape(n_seq, seq_len, N_HEADS),
            p3(ph)[:, seq_len - (CONV_W - 1):, :c],
            p3(oa_p)[:, -1],
            p3(pt)[:, seq_len - POOL_HIST:, :c]))
        s3 = lambda a: a.reshape(db, dt, -1)
        s_states.append((
            sh[:, 2 * c:3 * c].reshape(db, dt, *hd),
            sh[:, 3 * c:].reshape(db, dt, *hd),
            lf_s[:, :N_HEADS].reshape(db, dt, N_HEADS),
            jnp.concatenate([state_conv[l], s3(sh)[:, :, :c]], axis=1)[:, -(CONV_W - 1):],
            oa_s[dt - 1],
            jnp.concatenate([state_pool[l], s3(st)[:, :, :c]], axis=1)[:, -POOL_HIST:],
            vn_s.transpose(1, 0, 2)))

        xp, xpb = _channel_mix(l, depth, xp, xpb, (oa_p, ob_p, oc_p, od_p), w, tm_p, 512)
        xs, xsb = _channel_mix(l, depth, xs, xsb,
                               (bmaj(oa_s), ob_s, bmaj(oc_s), bmaj(od_s)), w, n_s, n_s)

    yp = xp.reshape(n_seq, seq_len, d)
    ys = xs.reshape(db, dt, d)
    p_out = [jnp.stack([s[j] for s in p_states]) for j in range(6)]
    s_out = [jnp.stack([s[j] for s in s_states]) for j in range(7)]
    return (yp, ys, *p_out, *s_out)
```
